```python
import math
import jax, jax.numpy as jnp
from jax import lax
import numpy as np

D_MODEL = 1024
BATCH = 32
SEQ = 2048
DEPTH = 2

D_FF = 2816
FFN_RESIDUAL_SCALE = 0.5
EPS = 1e-6
BLOCK_Q = 128

A_WIDTH = 512
A_CONV = 3
B_WIDTH = 512
B_CONV = 31
C_HEADS = 4
C_QK_DIM = 64
C_V_DIM = 2 * C_QK_DIM
C_QK_WIDTH = C_HEADS * 2 * C_QK_DIM
C_V_WIDTH = C_HEADS * C_V_DIM
D_HEADS = 8
D_HEAD_DIM = 64
D_WIDTH = D_HEADS * D_HEAD_DIM
N_BRANCHES = 4

IN_SPLITS = (A_WIDTH, A_WIDTH, A_WIDTH,
             2 * B_WIDTH,
             C_QK_WIDTH, C_QK_WIDTH, C_V_WIDTH,
             D_WIDTH, D_WIDTH, D_WIDTH, D_HEADS,
             N_BRANCHES * D_MODEL)
IN_WIDTH = sum(IN_SPLITS)

kernel_name = 'hybrid_parallel_mixer_block'


def _split(x, sizes):
    idx = []
    acc = 0
    for s in sizes[:-1]:
        acc += s
        idx.append(acc)
    return jnp.split(x, idx, axis=-1)


def rms_norm(x, g):
    xf = x.astype(jnp.float32)
    y = xf * lax.rsqrt(jnp.mean(xf * xf, axis=-1, keepdims=True) + EPS)
    return (y * g.astype(jnp.float32)).astype(x.dtype)


def layer_norm(x, g, b):
    xf = x.astype(jnp.float32)
    mu = jnp.mean(xf, axis=-1, keepdims=True)
    xc = xf - mu
    var = jnp.mean(xc * xc, axis=-1, keepdims=True)
    return (xc * lax.rsqrt(var + EPS) * g.astype(jnp.float32) + b.astype(jnp.float32)).astype(x.dtype)


def swiglu(h, w_gate, w_up, w_down):
    return (jax.nn.silu(h @ w_gate) * (h @ w_up)) @ w_down


def causal_depthwise_conv(x, w):
    k_width, ch = w.shape
    return lax.conv_general_dilated(x, w[:, None, :], window_strides=(1,), padding=[(k_width - 1, 0)],
                                    dimension_numbers=('NWC', 'WIO', 'NWC'), feature_group_count=ch)


def short_conv_mixer(gate_b, gate_c, xin, conv_w, w_out):
    return (gate_b * causal_depthwise_conv(gate_c * xin, conv_w)) @ w_out


def conformer_conv_mixer(u, conv_w, conv_b, ln_g, ln_b, w_out):
    a, g = jnp.split(u, 2, axis=-1)
    y = a * jax.nn.sigmoid(g)
    y = causal_depthwise_conv(y, conv_w) + conv_b
    y = jax.nn.silu(layer_norm(y, ln_g, ln_b))
    return y @ w_out


def alibi_slopes(n_heads):
    return jnp.exp2(-8.0 / n_heads * jnp.arange(1, n_heads + 1, dtype=jnp.float32))


def diff_attention_mixer(q, k, v, lam_q1, lam_k1, lam_q2, lam_k2, subln_g, w_out, lambda_init):
    b, s, _ = q.shape
    q = q.reshape(b, s, C_HEADS, 2, C_QK_DIM)
    k = k.reshape(b, s, C_HEADS, 2, C_QK_DIM)
    v = v.reshape(b, s, C_HEADS, C_V_DIM)
    f32 = jnp.float32
    lam = (jnp.exp(jnp.sum(lam_q1.astype(f32) * lam_k1.astype(f32)))
           - jnp.exp(jnp.sum(lam_q2.astype(f32) * lam_k2.astype(f32))) + lambda_init)
    slopes = alibi_slopes(C_HEADS)[:, None, None]
    scale = C_QK_DIM ** -0.5
    outs = []
    for i in range(s // BLOCK_Q):
        q0, q1 = i * BLOCK_Q, (i + 1) * BLOCK_Q
        dist = (jnp.arange(q0, q1)[:, None] - jnp.arange(q1)[None, :]).astype(f32)
        bias = jnp.where(dist >= 0, -slopes * dist, -jnp.inf)
        logits = jnp.einsum('bqhmd,bkhmd->bmhqk', q[:, q0:q1], k[:, :q1]).astype(f32) * scale + bias
        p = jax.nn.softmax(logits, axis=-1)
        pd = p[:, 0] - lam * p[:, 1]
        outs.append(jnp.einsum('bhqk,bkhd->bqhd', pd.astype(v.dtype), v[:, :q1]))
    o = jnp.concatenate(outs, axis=1)
    o = rms_norm(o, subln_g) * (1.0 - lambda_init)
    return o.reshape(b, s, C_V_WIDTH) @ w_out


def forgetting_attention_mixer(q, k, v, f_logit, f_bias, w_out):
    b, s, _ = q.shape
    q = q.reshape(b, s, D_HEADS, D_HEAD_DIM)
    k = k.reshape(b, s, D_HEADS, D_HEAD_DIM)
    v = v.reshape(b, s, D_HEADS, D_HEAD_DIM)
    f32 = jnp.float32
    log_f = jax.nn.log_sigmoid(f_logit.astype(f32) + f_bias.astype(f32))
    cum = jnp.cumsum(log_f, axis=1).transpose(0, 2, 1)
    scale = D_HEAD_DIM ** -0.5
    outs = []
    for i in range(s // BLOCK_Q):
        q0, q1 = i * BLOCK_Q, (i + 1) * BLOCK_Q
        causal = jnp.arange(q0, q1)[:, None] >= jnp.arange(q1)[None, :]
        decay = cum[:, :, q0:q1, None] - cum[:, :, None, :q1]
        bias = jnp.where(causal, decay, -jnp.inf)
        logits = jnp.einsum('bqhd,bkhd->bhqk', q[:, q0:q1], k[:, :q1]).astype(f32) * scale + bias
        p = jax.nn.softmax(logits, axis=-1)
        outs.append(jnp.einsum('bhqk,bkhd->bqhd', p.astype(v.dtype), v[:, :q1]))
    o = jnp.concatenate(outs, axis=1)
    return o.reshape(b, s, D_WIDTH) @ w_out


def setup_inputs(seed: int = 0) -> dict:
    key = jax.random.key(seed)
    ks = jax.random.split(key, 40)
    it = iter(range(40))

    def nk():
        return ks[next(it)]

    def w(shape, fan_in):
        return jax.random.normal(nk(), shape, jnp.float32) * (fan_in ** -0.5)

    def gain(shape):
        return 1.0 + 0.05 * jax.random.normal(nk(), shape, jnp.float32)

    def small(shape, scale=0.02, offset=0.0):
        return offset + scale * jax.random.normal(nk(), shape, jnp.float32)

    L = DEPTH
    return {
        'x': jax.random.normal(nk(), (BATCH, SEQ, D_MODEL), jnp.float32),
        'ffn1_pre_g': gain((L, D_MODEL)),
        'ffn1_post_g': gain((L, D_MODEL)),
        'ffn1_w_gate': w((L, D_MODEL, D_FF), D_MODEL),
        'ffn1_w_up': w((L, D_MODEL, D_FF), D_MODEL),
        'ffn1_w_down': w((L, D_FF, D_MODEL), D_FF),
        'mix_pre_g': gain((L, D_MODEL)),
        'mix_post_g': gain((L, D_MODEL)),
        'w_in': w((L, D_MODEL, IN_WIDTH), D_MODEL),
        'a_conv_w': w((L, A_CONV, A_WIDTH), A_CONV),
        'a_w_out': w((L, A_WIDTH, D_MODEL), A_WIDTH),
        'b_conv_w': w((L, B_CONV, B_WIDTH), B_CONV),
        'b_conv_b': small((L, B_WIDTH)),
        'b_ln_g': gain((L, B_WIDTH)),
        'b_ln_b': small((L, B_WIDTH)),
        'b_w_out': w((L, B_WIDTH, D_MODEL), B_WIDTH),
        'c_lam_q1': small((L, C_QK_DIM), 0.1),
        'c_lam_k1': small((L, C_QK_DIM), 0.1),
        'c_lam_q2': small((L, C_QK_DIM), 0.1),
        'c_lam_k2': small((L, C_QK_DIM), 0.1),
        'c_subln_g': gain((L, C_V_DIM)),
        'c_w_out': w((L, C_V_WIDTH, D_MODEL), C_V_WIDTH),
        'd_forget_b': small((L, D_HEADS), 0.5, 3.0),
        'd_w_out': w((L, D_WIDTH, D_MODEL), D_WIDTH),
        'w_o': w((L, D_MODEL, D_MODEL), D_MODEL),
        'ffn2_pre_g': gain((L, D_MODEL)),
        'ffn2_post_g': gain((L, D_MODEL)),
        'ffn2_w_gate': w((L, D_MODEL, D_FF), D_MODEL),
        'ffn2_w_up': w((L, D_MODEL, D_FF), D_MODEL),
        'ffn2_w_down': w((L, D_FF, D_MODEL), D_FF),
    }


def reference(x, ffn1_pre_g, ffn1_post_g, ffn1_w_gate, ffn1_w_up, ffn1_w_down,
              mix_pre_g, mix_post_g, w_in, a_conv_w, a_w_out,
              b_conv_w, b_conv_b, b_ln_g, b_ln_b, b_w_out,
              c_lam_q1, c_lam_k1, c_lam_q2, c_lam_k2, c_subln_g, c_w_out,
              d_forget_b, d_w_out, w_o,
              ffn2_pre_g, ffn2_post_g, ffn2_w_gate, ffn2_w_up, ffn2_w_down):
    bsz, seq, _ = x.shape
    h = x
    for l in range(DEPTH):
        lambda_init = 0.8 - 0.6 * math.exp(-0.3 * l)
        f = swiglu(rms_norm(h, ffn1_pre_g[l]), ffn1_w_gate[l], ffn1_w_up[l], ffn1_w_down[l])
        h = h + FFN_RESIDUAL_SCALE * rms_norm(f, ffn1_post_g[l])
        u = rms_norm(h, mix_pre_g[l])
        (a_b, a_c, a_x, b_u, c_q, c_k, c_v,
         d_q, d_k, d_v, d_f, gate_logits) = _split(u @ w_in[l], IN_SPLITS)
        y_a = short_conv_mixer(a_b, a_c, a_x, a_conv_w[l], a_w_out[l])
        y_b = conformer_conv_mixer(b_u, b_conv_w[l], b_conv_b[l], b_ln_g[l], b_ln_b[l], b_w_out[l])
        y_c = diff_attention_mixer(c_q, c_k, c_v, c_lam_q1[l], c_lam_k1[l], c_lam_q2[l], c_lam_k2[l],
                                   c_subln_g[l], c_w_out[l], lambda_init)
        y_d = forgetting_attention_mixer(d_q, d_k, d_v, d_f, d_forget_b[l], d_w_out[l])
        g = jax.nn.sigmoid(gate_logits).reshape(bsz, seq, N_BRANCHES, D_MODEL)
        merged = g[:, :, 0] * y_a + g[:, :, 1] * y_b + g[:, :, 2] * y_c + g[:, :, 3] * y_d
        h = h + rms_norm(merged @ w_o[l], mix_post_g[l])
        f = swiglu(rms_norm(h, ffn2_pre_g[l]), ffn2_w_gate[l], ffn2_w_up[l], ffn2_w_down[l])
        h = h + FFN_RESIDUAL_SCALE * rms_norm(f, ffn2_post_g[l])
    return h
```

```python
import functools
import math

import jax
import jax.numpy as jnp
from jax import lax
from jax.experimental import pallas as pl
from jax.experimental.pallas import tpu as pltpu

F32 = jnp.float32
BF16 = jnp.bfloat16

EPS = 1e-6
FFN_RESIDUAL_SCALE = 0.5
D_FF = 2816
A_WIDTH = 512
A_CONV = 3
B_WIDTH = 512
B_CONV = 31
C_HEADS = 4
C_QK_DIM = 64
C_V_DIM = 128
D_HEADS = 8
D_HEAD_DIM = 64
N_BRANCHES = 4

LANES = 128
SUBLANES = 8
MXU_DIM = 256
VMEM_LIMIT_BYTES = 56 * 1024 * 1024

TOKEN_TILE = 512
Q_BLOCK = MXU_DIM
BIAS_PIECES = 3
A_HIST = SUBLANES
B_HIST = 32
CONV_ROWS = 64
V_ROWS = C_V_DIM + 16
NEG_BIG = -1e30


def _rms(x, g):
    return x * lax.rsqrt(jnp.mean(x * x, axis=-1, keepdims=True) + EPS) * g


def _split3(x):
    hi = x.astype(BF16).astype(F32)
    r = x - hi
    mid = r.astype(BF16).astype(F32)
    lo = (r - mid).astype(BF16).astype(F32)
    return hi, mid, lo


def _piece_slab(x, lane):
    hi, mid, lo = _split3(x)
    r = lane % BIAS_PIECES
    return jnp.where(r == 0, hi, jnp.where(r == 1, mid, lo)).astype(BF16)


def _resident(shape):
    nd = len(shape)
    return pl.BlockSpec(shape, lambda *_: (0,) * nd, pipeline_mode=pl.Buffered(1))


def _params(n_axes):
    return pltpu.CompilerParams(
        dimension_semantics=("arbitrary",) * n_axes,
        vmem_limit_bytes=VMEM_LIMIT_BYTES,
    )


def _ffn_kernel(emit_norm, x_ref, pre_g_ref, post_g_ref, wg_ref, wu_ref, wd_ref, *rest):
    if emit_norm:
        next_g_ref, h_ref, u_ref = rest
    else:
        (h_ref,) = rest
    x = x_ref[...]
    xn = _rms(x, pre_g_ref[...]).astype(BF16)
    g = jnp.dot(xn, wg_ref[...], preferred_element_type=F32)
    u = jnp.dot(xn, wu_ref[...], preferred_element_type=F32)
    a = (g * jax.nn.sigmoid(g) * u).astype(BF16)
    f = jnp.dot(a, wd_ref[...], preferred_element_type=F32)
    h = x + FFN_RESIDUAL_SCALE * _rms(f, post_g_ref[...])
    h_ref[...] = h
    if emit_norm:
        u_ref[...] = _rms(h, next_g_ref[...]).astype(BF16)


def _ffn(x, pre_g, post_g, wg, wu, wd, next_g=None):
    t, d = x.shape
    tm = min(TOKEN_TILE, t)
    emit_norm = next_g is not None
    row = pl.BlockSpec((tm, d), lambda i: (i, 0))
    vec = _resident((1, d))
    in_specs = [row, vec, vec, _resident(wg.shape), _resident(wu.shape), _resident(wd.shape)]
    args = [x, pre_g, post_g, wg, wu, wd]
    out_shape = [jax.ShapeDtypeStruct((t, d), F32)]
    out_specs = [row]
    if emit_norm:
        in_specs.append(vec)
        args.append(next_g)
        out_shape.append(jax.ShapeDtypeStruct((t, d), BF16))
        out_specs.append(row)
    out = pl.pallas_call(
        functools.partial(_ffn_kernel, emit_norm),
        grid=(t // tm,),
        in_specs=in_specs,
        out_specs=out_specs,
        out_shape=out_shape,
        compiler_params=_params(1),
        name="ffn_norm" if emit_norm else "ffn",
    )(*args)
    return out if emit_norm else out[0]


def _causal_conv(buf_ref, hist, w, n_taps, tm):
    chunks = []
    for r0 in range(0, tm, CONV_ROWS):
        acc = None
        for k in range(n_taps):
            start = hist + r0 - (n_taps - 1) + k
            term = buf_ref[start:start + CONV_ROWS, :] * w[k:k + 1, :]
            acc = term if acc is None else acc + term
        chunks.append(acc)
    return jnp.concatenate(chunks, axis=0)


def _ab_kernel(tiles_per_seq, u_ref, wa_ref, wb_ref, aw_ref, bw_ref, bb_ref, lng_ref, lnb_ref,
               o_ref, zbuf, ybuf):
    tm = u_ref.shape[0]
    first = (pl.program_id(0) % tiles_per_seq) == 0

    @pl.when(first)
    def _():
        zbuf[0:A_HIST, :] = jnp.zeros((A_HIST, A_WIDTH), F32)
        ybuf[0:B_HIST, :] = jnp.zeros((B_HIST, B_WIDTH), F32)

    @pl.when(jnp.logical_not(first))
    def _():
        zbuf[0:A_HIST, :] = zbuf[tm:tm + A_HIST, :]
        ybuf[0:B_HIST, :] = ybuf[tm:tm + B_HIST, :]

    u = u_ref[...]
    pa = jnp.dot(u, wa_ref[...], preferred_element_type=F32)
    gate_b = pa[:, 0:A_WIDTH]
    zbuf[A_HIST:A_HIST + tm, :] = pa[:, A_WIDTH:2 * A_WIDTH] * pa[:, 2 * A_WIDTH:3 * A_WIDTH]
    o_a = gate_b * _causal_conv(zbuf, A_HIST, aw_ref[...], A_CONV, tm)

    pb = jnp.dot(u, wb_ref[...], preferred_element_type=F32)
    ybuf[B_HIST:B_HIST + tm, :] = pb[:, 0:B_WIDTH] * jax.nn.sigmoid(pb[:, B_WIDTH:2 * B_WIDTH])
    y = _causal_conv(ybuf, B_HIST, bw_ref[...], B_CONV, tm) + bb_ref[...]
    mu = jnp.mean(y, axis=-1, keepdims=True)
    yc = y - mu
    var = jnp.mean(yc * yc, axis=-1, keepdims=True)
    yn = yc * lax.rsqrt(var + EPS) * lng_ref[...] + lnb_ref[...]
    o_b = yn * jax.nn.sigmoid(yn)

    o_ref[:, 0:A_WIDTH] = o_a.astype(BF16)
    o_ref[:, A_WIDTH:A_WIDTH + B_WIDTH] = o_b.astype(BF16)


def _ab(u, seq, wa, wb, a_conv_w, b_conv_w, b_conv_b, b_ln_g, b_ln_b):
    t, d = u.shape
    tm = min(TOKEN_TILE, seq)
    return pl.pallas_call(
        functools.partial(_ab_kernel, seq // tm),
        grid=(t // tm,),
        in_specs=[
            pl.BlockSpec((tm, d), lambda i: (i, 0)),
            _resident(wa.shape), _resident(wb.shape),
            _resident(a_conv_w.shape), _resident(b_conv_w.shape),
            _resident(b_conv_b.shape), _resident(b_ln_g.shape), _resident(b_ln_b.shape),
        ],
        out_specs=pl.BlockSpec((tm, A_WIDTH + B_WIDTH), lambda i: (i, 0)),
        out_shape=jax.ShapeDtypeStruct((t, A_WIDTH + B_WIDTH), BF16),
        scratch_shapes=[
            pltpu.VMEM((A_HIST + tm, A_WIDTH), F32),
            pltpu.VMEM((B_HIST + tm, B_WIDTH), F32),
        ],
        compiler_params=_params(1),
        name="conv_branches",
    )(u, wa, wb, a_conv_w, b_conv_w, b_conv_b, b_ln_g, b_ln_b)


def _attend(kaug_ref, qx_ref, vt_ref, mask_bias, j):
    q0 = j * Q_BLOCK
    w = q0 + Q_BLOCK
    s = lax.dot_general(kaug_ref[0:w, :], qx_ref[q0:w, :], (((1,), (1,)), ((), ())),
                        preferred_element_type=F32)
    s_diag = s[q0:w, :] + mask_bias
    m = jnp.max(s_diag, axis=0, keepdims=True)
    if j > 0:
        s_top = s[0:q0, :]
        m = jnp.maximum(m, jnp.max(s_top, axis=0, keepdims=True))
        p = jnp.concatenate([jnp.exp(s_top - m), jnp.exp(s_diag - m)], axis=0)
    else:
        p = jnp.exp(s_diag - m)
    return jnp.dot(vt_ref[:, 0:w], p.astype(BF16), preferred_element_type=F32)


def _causal_mask_bias():
    key = lax.broadcasted_iota(jnp.int32, (Q_BLOCK, Q_BLOCK), 0)
    qry = lax.broadcasted_iota(jnp.int32, (Q_BLOCK, Q_BLOCK), 1)
    return jnp.where(key <= qry, 0.0, NEG_BIG).astype(F32)


def _init_vt_tail(vt_ref):
    seq = vt_ref.shape[1]
    row = lax.broadcasted_iota(jnp.int32, (V_ROWS - C_V_DIM, seq), 0)
    vt_ref[C_V_DIM:V_ROWS, :] = jnp.where(row == 0, 1.0, 0.0).astype(BF16)


def _ones_slab(seq, first_lane):
    lane = lax.broadcasted_iota(jnp.int32, (seq, LANES), 1)
    hit = jnp.logical_and(lane >= first_lane, lane < first_lane + BIAS_PIECES)
    return jnp.where(hit, 1.0, 0.0).astype(BF16)


def _attn_c_kernel(lambda_init, u_ref, w_ref, slope_ref, lam_ref, subg_ref, o_ref,
                   kaug, qx0, qx1, vt, ot):
    seq = u_ref.shape[0]
    b = pl.program_id(0)
    head = pl.program_id(1)
    half = C_QK_DIM

    @pl.when(jnp.logical_and(b == 0, head == 0))
    def _():
        _init_vt_tail(vt)
        pos = lax.broadcasted_iota(jnp.int32, (seq, LANES), 0).astype(F32)
        lane = lax.broadcasted_iota(jnp.int32, (seq, LANES), 1)
        kaug[:, LANES:2 * LANES] = _piece_slab(pos * slope_ref[...], lane)

    qkv = jnp.dot(u_ref[...], w_ref[0], preferred_element_type=F32)
    q = qkv[:, 0:LANES] * (C_QK_DIM ** -0.5)
    lane = lax.broadcasted_iota(jnp.int32, (seq, LANES), 1)
    ones = _ones_slab(seq, BIAS_PIECES * head)
    qx0[:, 0:LANES] = jnp.where(lane < half, q, 0.0).astype(BF16)
    qx0[:, LANES:2 * LANES] = ones
    qx1[:, 0:LANES] = jnp.where(lane >= half, q, 0.0).astype(BF16)
    qx1[:, LANES:2 * LANES] = ones
    kaug[:, 0:LANES] = qkv[:, LANES:2 * LANES].astype(BF16)
    vt[0:C_V_DIM, :] = qkv[:, 2 * LANES:3 * LANES].T.astype(BF16)

    lp = lam_ref[...]
    lam = (jnp.exp(jnp.sum(lp[0:1, :] * lp[1:2, :], keepdims=True))
           - jnp.exp(jnp.sum(lp[2:3, :] * lp[3:4, :], keepdims=True)) + lambda_init)
    mask_bias = _causal_mask_bias()
    for j in range(seq // Q_BLOCK):
        o0 = _attend(kaug, qx0, vt, mask_bias, j)
        o1 = _attend(kaug, qx1, vt, mask_bias, j)
        d = (o0[0:C_V_DIM, :] / o0[C_V_DIM:C_V_DIM + 1, :]
             - lam * (o1[0:C_V_DIM, :] / o1[C_V_DIM:C_V_DIM + 1, :]))
        ot[:, j * Q_BLOCK:(j + 1) * Q_BLOCK] = d
    o = ot[...].T
    o_ref[...] = (_rms(o, subg_ref[...]) * (1.0 - lambda_init)).astype(BF16)


def _attn_d_kernel(u_ref, w_ref, fb_ref, o_ref, kaug, qx0, qx1, vt, ot):
    seq = u_ref.shape[0]
    b = pl.program_id(0)
    pair = pl.program_id(1)
    half = D_HEAD_DIM

    @pl.when(jnp.logical_and(b == 0, pair == 0))
    def _():
        _init_vt_tail(vt)

    qkv = jnp.dot(u_ref[...], w_ref[0], preferred_element_type=F32)

    @pl.when(pair == 0)
    def _():
        x = qkv[:, 3 * LANES:4 * LANES] + fb_ref[...]
        logf = jnp.minimum(x, 0.0) - jnp.log1p(jnp.exp(-jnp.abs(x)))
        row = lax.broadcasted_iota(jnp.int32, (seq, LANES), 0)
        c = logf
        shift = 1
        while shift < seq:
            c = c + jnp.where(row >= shift, pltpu.roll(c, shift, axis=0), 0.0)
            shift *= 2
        lane = lax.broadcasted_iota(jnp.int32, (seq, LANES), 1)
        kaug[:, LANES:2 * LANES] = _piece_slab(-c, lane)

    q = qkv[:, 0:LANES] * (D_HEAD_DIM ** -0.5)
    lane = lax.broadcasted_iota(jnp.int32, (seq, LANES), 1)
    qx0[:, 0:LANES] = jnp.where(lane < half, q, 0.0).astype(BF16)
    qx0[:, LANES:2 * LANES] = _ones_slab(seq, BIAS_PIECES * 2 * pair)
    qx1[:, 0:LANES] = jnp.where(lane >= half, q, 0.0).astype(BF16)
    qx1[:, LANES:2 * LANES] = _ones_slab(seq, BIAS_PIECES * (2 * pair + 1))
    kaug[:, 0:LANES] = qkv[:, LANES:2 * LANES].astype(BF16)
    vt[0:C_V_DIM, :] = qkv[:, 2 * LANES:3 * LANES].T.astype(BF16)

    mask_bias = _causal_mask_bias()
    vrow = lax.broadcasted_iota(jnp.int32, (C_V_DIM, Q_BLOCK), 0)
    for j in range(seq // Q_BLOCK):
        o0 = _attend(kaug, qx0, vt, mask_bias, j)
        o1 = _attend(kaug, qx1, vt, mask_bias, j)
        ot[:, j * Q_BLOCK:(j + 1) * Q_BLOCK] = jnp.where(
            vrow < half,
            o0[0:C_V_DIM, :] / o0[C_V_DIM:C_V_DIM + 1, :],
            o1[0:C_V_DIM, :] / o1[C_V_DIM:C_V_DIM + 1, :])
    o_ref[...] = ot[...].T.astype(BF16)


def _attn_scratch(seq):
    return [
        pltpu.VMEM((seq, 2 * LANES), BF16),
        pltpu.VMEM((seq, 2 * LANES), BF16),
        pltpu.VMEM((seq, 2 * LANES), BF16),
        pltpu.VMEM((V_ROWS, seq), BF16),
        pltpu.VMEM((C_V_DIM, seq), F32),
    ]


def _attn_c(u, seq, w, slope_vec, lam_params, subln_g, lambda_init):
    t, d = u.shape
    return pl.pallas_call(
        functools.partial(_attn_c_kernel, lambda_init),
        grid=(t // seq, C_HEADS),
        in_specs=[
            pl.BlockSpec((seq, d), lambda b, h: (b, 0)),
            pl.BlockSpec((1,) + w.shape[1:], lambda b, h: (h, 0, 0)),
            _resident(slope_vec.shape), _resident(lam_params.shape), _resident(subln_g.shape),
        ],
        out_specs=pl.BlockSpec((seq, LANES), lambda b, h: (b, h)),
        out_shape=jax.ShapeDtypeStruct((t, C_HEADS * C_V_DIM), BF16),
        scratch_shapes=_attn_scratch(seq),
        compiler_params=_params(2),
        name="diff_attention",
    )(u, w, slope_vec, lam_params, subln_g)


def _attn_d(u, seq, w, fbias_vec):
    t, d = u.shape
    n_pairs = D_HEADS // 2
    return pl.pallas_call(
        _attn_d_kernel,
        grid=(t // seq, n_pairs),
        in_specs=[
            pl.BlockSpec((seq, d), lambda b, p: (b, 0)),
            pl.BlockSpec((1,) + w.shape[1:], lambda b, p: (p, 0, 0)),
            _resident(fbias_vec.shape),
        ],
        out_specs=pl.BlockSpec((seq, LANES), lambda b, p: (b, p)),
        out_shape=jax.ShapeDtypeStruct((t, D_HEADS * D_HEAD_DIM), BF16),
        scratch_shapes=_attn_scratch(seq),
        compiler_params=_params(2),
        name="forget_attention",
    )(u, w, fbias_vec)


def _merge_kernel(u_ref, h_ref, oab_ref, oc_ref, od_ref, wg_ref, wa_ref, wb_ref, wc_ref, wd_ref,
                  wo_ref, post_g_ref, out_ref):
    d = h_ref.shape[1]
    u = u_ref[...]
    branches = (
        (oab_ref[:, 0:A_WIDTH], wa_ref),
        (oab_ref[:, A_WIDTH:A_WIDTH + B_WIDTH], wb_ref),
        (oc_ref[...], wc_ref),
        (od_ref[...], wd_ref),
    )
    merged = None
    for i, (o, w_ref) in enumerate(branches):
        gate = jax.nn.sigmoid(jnp.dot(u, wg_ref[:, i * d:(i + 1) * d], preferred_element_type=F32))
        term = gate * jnp.dot(o, w_ref[...], preferred_element_type=F32)
        merged = term if merged is None else merged + term
    y = jnp.dot(merged.astype(BF16), wo_ref[...], preferred_element_type=F32)
    out_ref[...] = h_ref[...] + _rms(y, post_g_ref[...])


def _merge(u, h, oab, oc, od, wg, wa, wb, wc, wd, wo, post_g):
    t, d = h.shape
    tm = min(TOKEN_TILE, t)

    def row(width):
        return pl.BlockSpec((tm, width), lambda i: (i, 0))

    return pl.pallas_call(
        _merge_kernel,
        grid=(t // tm,),
        in_specs=[
            row(d), row(d), row(oab.shape[1]), row(oc.shape[1]), row(od.shape[1]),
            _resident(wg.shape), _resident(wa.shape), _resident(wb.shape), _resident(wc.shape),
            _resident(wd.shape), _resident(wo.shape), _resident(post_g.shape),
        ],
        out_specs=row(d),
        out_shape=jax.ShapeDtypeStruct((t, d), F32),
        compiler_params=_params(1),
        name="gated_merge",
    )(u, h, oab, oc, od, wg, wa, wb, wc, wd, wo, post_g)


def _column_offsets(d_model):
    sizes = (A_WIDTH, A_WIDTH, A_WIDTH, 2 * B_WIDTH,
             C_HEADS * 2 * C_QK_DIM, C_HEADS * 2 * C_QK_DIM, C_HEADS * C_V_DIM,
             D_HEADS * D_HEAD_DIM, D_HEADS * D_HEAD_DIM, D_HEADS * D_HEAD_DIM, D_HEADS,
             N_BRANCHES * d_model)
    offs = [0]
    for s in sizes:
        offs.append(offs[-1] + s)
    return offs


def _per_group(w, n_groups):
    d = w.shape[0]
    return w.reshape(d, n_groups, LANES).transpose(1, 0, 2)


def _replicated_lanes(vals):
    rep = jnp.repeat(vals, BIAS_PIECES, axis=-1)
    pad = [(0, 0)] * (rep.ndim - 1) + [(0, LANES - rep.shape[-1])]
    return jnp.pad(rep, pad)


def kernel(x, ffn1_pre_g, ffn1_post_g, ffn1_w_gate, ffn1_w_up, ffn1_w_down, mix_pre_g, mix_post_g, w_in, a_conv_w, a_w_out, b_conv_w, b_conv_b, b_ln_g, b_ln_b, b_w_out, c_lam_q1, c_lam_k1, c_lam_q2, c_lam_k2, c_subln_g, c_w_out, d_forget_b, d_w_out, w_o, ffn2_pre_g, ffn2_post_g, ffn2_w_gate, ffn2_w_up, ffn2_w_down):
    bsz, seq, d = x.shape
    depth = w_in.shape[0]
    assert seq % Q_BLOCK == 0 and seq % min(TOKEN_TILE, seq) == 0
    offs = _column_offsets(d)
    h = x.reshape(bsz * seq, d)

    def vec(v):
        return v.reshape(1, -1).astype(F32)

    slope_vec = _replicated_lanes(
        jnp.exp2(-8.0 / C_HEADS * jnp.arange(1, C_HEADS + 1, dtype=F32)))[None, :]

    for l in range(depth):
        lambda_init = 0.8 - 0.6 * math.exp(-0.3 * l)
        wl = w_in[l].astype(BF16)

        def cols(i, j=None):
            return wl[:, offs[i]:offs[i + 1 if j is None else j]]

        w_a = cols(0, 3)
        w_b = cols(3)
        w_c = jnp.concatenate([_per_group(cols(4), C_HEADS), _per_group(cols(5), C_HEADS),
                               _per_group(cols(6), C_HEADS)], axis=-1)
        n_pairs = D_HEADS // 2
        f_slab = jnp.broadcast_to(_replicated_lanes(cols(10))[None], (n_pairs, d, LANES))
        w_d = jnp.concatenate([_per_group(cols(7), n_pairs), _per_group(cols(8), n_pairs),
                               _per_group(cols(9), n_pairs), f_slab], axis=-1)
        w_g = cols(11)

        h, u = _ffn(h, vec(ffn1_pre_g[l]), vec(ffn1_post_g[l]), ffn1_w_gate[l].astype(BF16),
                    ffn1_w_up[l].astype(BF16), ffn1_w_down[l].astype(BF16), vec(mix_pre_g[l]))
        oab = _ab(u, seq, w_a, w_b, a_conv_w[l], b_conv_w[l], vec(b_conv_b[l]), vec(b_ln_g[l]),
                  vec(b_ln_b[l]))
        lam_params = jnp.stack([c_lam_q1[l], c_lam_k1[l], c_lam_q2[l], c_lam_k2[l]]).astype(F32)
        oc = _attn_c(u, seq, w_c, slope_vec, lam_params, vec(c_subln_g[l]), lambda_init)
        od = _attn_d(u, seq, w_d, _replicated_lanes(d_forget_b[l].astype(F32))[None, :])
        h = _merge(u, h, oab, oc, od, w_g, a_w_out[l].astype(BF16), b_w_out[l].astype(BF16),
                   c_w_out[l].astype(BF16), d_w_out[l].astype(BF16), w_o[l].astype(BF16),
                   vec(mix_post_g[l]))
        h = _ffn(h, vec(ffn2_pre_g[l]), vec(ffn2_post_g[l]), ffn2_w_gate[l].astype(BF16),
                 ffn2_w_up[l].astype(BF16), ffn2_w_down[l].astype(BF16))
    return h.reshape(bsz, seq, d)
```

```python
import functools
import math

import jax
import jax.numpy as jnp
from jax import lax
from jax.experimental import pallas as pl
from jax.experimental.pallas import tpu as pltpu

F32 = jnp.float32
BF16 = jnp.bfloat16

EPS = 1e-6
FFN_RESIDUAL_SCALE = 0.5
D_FF = 2816
A_WIDTH = 512
A_CONV = 3
B_WIDTH = 512
B_CONV = 31
C_HEADS = 4
C_QK_DIM = 64
C_V_DIM = 128
D_HEADS = 8
D_HEAD_DIM = 64
N_BRANCHES = 4

LANES = 128
SUBLANES = 8
MXU_DIM = 256
VMEM_LIMIT_BYTES = 56 * 1024 * 1024

TOKEN_TILE = 512
Q_BLOCK = MXU_DIM
BIAS_PIECES = 3
A_HIST = SUBLANES
B_HIST = 32
CONV_ROWS = 64
V_ROWS = C_V_DIM + 16
NEG_BIG = -1e30
LOG2_E = math.log2(math.e)


def _rms(x, g):
    return x * lax.rsqrt(jnp.mean(x * x, axis=-1, keepdims=True) + EPS) * g


def _split3(x):
    hi = x.astype(BF16).astype(F32)
    r = x - hi
    mid = r.astype(BF16).astype(F32)
    lo = (r - mid).astype(BF16).astype(F32)
    return hi, mid, lo


def _piece_slab(x, lane):
    hi, mid, lo = _split3(x)
    r = lane % BIAS_PIECES
    return jnp.where(r == 0, hi, jnp.where(r == 1, mid, lo)).astype(BF16)


def _resident(shape):
    nd = len(shape)
    return pl.BlockSpec(shape, lambda *_: (0,) * nd, pipeline_mode=pl.Buffered(1))


def _params(n_axes):
    return pltpu.CompilerParams(
        dimension_semantics=("arbitrary",) * n_axes,
        vmem_limit_bytes=VMEM_LIMIT_BYTES,
    )


def _ffn_kernel(emit_norm, x_ref, pre_g_ref, post_g_ref, wg_ref, wu_ref, wd_ref, *rest):
    if emit_norm:
        next_g_ref, h_ref, u_ref = rest
    else:
        (h_ref,) = rest
    x = x_ref[...]
    xn = _rms(x, pre_g_ref[...]).astype(BF16)
    g = jnp.dot(xn, wg_ref[...], preferred_element_type=F32)
    u = jnp.dot(xn, wu_ref[...], preferred_element_type=F32)
    a = (g * jax.nn.sigmoid(g) * u).astype(BF16)
    f = jnp.dot(a, wd_ref[...], preferred_element_type=F32)
    h = x + FFN_RESIDUAL_SCALE * _rms(f, post_g_ref[...])
    h_ref[...] = h
    if emit_norm:
        u_ref[...] = _rms(h, next_g_ref[...]).astype(BF16)


def _ffn(x, pre_g, post_g, wg, wu, wd, next_g=None):
    t, d = x.shape
    tm = min(TOKEN_TILE, t)
    emit_norm = next_g is not None
    row = pl.BlockSpec((tm, d), lambda i: (i, 0))
    vec = _resident((1, d))
    in_specs = [row, vec, vec, _resident(wg.shape), _resident(wu.shape), _resident(wd.shape)]
    args = [x, pre_g, post_g, wg, wu, wd]
    out_shape = [jax.ShapeDtypeStruct((t, d), F32)]
    out_specs = [row]
    if emit_norm:
        in_specs.append(vec)
        args.append(next_g)
        out_shape.append(jax.ShapeDtypeStruct((t, d), BF16))
        out_specs.append(row)
    out = pl.pallas_call(
        functools.partial(_ffn_kernel, emit_norm),
        grid=(t // tm,),
        in_specs=in_specs,
        out_specs=out_specs,
        out_shape=out_shape,
        compiler_params=_params(1),
        name="ffn_norm" if emit_norm else "ffn",
    )(*args)
    return out if emit_norm else out[0]


def _causal_conv(buf_ref, hist, w, n_taps, tm):
    chunks = []
    for r0 in range(0, tm, CONV_ROWS):
        acc = None
        for k in range(n_taps):
            start = hist + r0 - (n_taps - 1) + k
            term = buf_ref[start:start + CONV_ROWS, :] * w[k:k + 1, :]
            acc = term if acc is None else acc + term
        chunks.append(acc)
    return jnp.concatenate(chunks, axis=0)


def _causal_conv_aligned(buf_ref, shift_ref, hist, w, n_taps, tm):
    rows = shift_ref.shape[1]
    for r in range(1, SUBLANES):
        shift_ref[r - 1] = buf_ref[r:r + rows, :]
    chunks = []
    for r0 in range(0, tm, CONV_ROWS):
        acc = None
        for k in range(n_taps):
            base = hist - (n_taps - 1) + k
            phase = base % SUBLANES
            start = base - phase + r0
            if phase == 0:
                x = buf_ref[start:start + CONV_ROWS, :]
            else:
                x = shift_ref[phase - 1, start:start + CONV_ROWS, :]
            term = x * w[k:k + 1, :]
            acc = term if acc is None else acc + term
        chunks.append(acc)
    return jnp.concatenate(chunks, axis=0)


def _ab_kernel(tiles_per_seq, u_ref, wa_ref, wb_ref, aw_ref, bw_ref, bb_ref, lng_ref, lnb_ref,
               o_ref, zbuf, ybuf, yshift):
    tm = u_ref.shape[0]
    first = (pl.program_id(0) % tiles_per_seq) == 0

    @pl.when(first)
    def _():
        zbuf[0:A_HIST, :] = jnp.zeros((A_HIST, A_WIDTH), F32)
        ybuf[0:B_HIST, :] = jnp.zeros((B_HIST, B_WIDTH), F32)

    @pl.when(jnp.logical_not(first))
    def _():
        zbuf[0:A_HIST, :] = zbuf[tm:tm + A_HIST, :]
        ybuf[0:B_HIST, :] = ybuf[tm:tm + B_HIST, :]

    u = u_ref[...]
    pa = jnp.dot(u, wa_ref[...], preferred_element_type=F32)
    gate_b = pa[:, 0:A_WIDTH]
    zbuf[A_HIST:A_HIST + tm, :] = pa[:, A_WIDTH:2 * A_WIDTH] * pa[:, 2 * A_WIDTH:3 * A_WIDTH]
    o_a = gate_b * _causal_conv(zbuf, A_HIST, aw_ref[...], A_CONV, tm)

    pb = jnp.dot(u, wb_ref[...], preferred_element_type=F32)
    ybuf[B_HIST:B_HIST + tm, :] = pb[:, 0:B_WIDTH] * jax.nn.sigmoid(pb[:, B_WIDTH:2 * B_WIDTH])
    y = _causal_conv_aligned(ybuf, yshift, B_HIST, bw_ref[...], B_CONV, tm) + bb_ref[...]
    mu = jnp.mean(y, axis=-1, keepdims=True)
    yc = y - mu
    var = jnp.mean(yc * yc, axis=-1, keepdims=True)
    yn = yc * lax.rsqrt(var + EPS) * lng_ref[...] + lnb_ref[...]
    o_b = yn * jax.nn.sigmoid(yn)

    o_ref[:, 0:A_WIDTH] = o_a.astype(BF16)
    o_ref[:, A_WIDTH:A_WIDTH + B_WIDTH] = o_b.astype(BF16)


def _ab(u, seq, wa, wb, a_conv_w, b_conv_w, b_conv_b, b_ln_g, b_ln_b):
    t, d = u.shape
    tm = min(TOKEN_TILE, seq)
    return pl.pallas_call(
        functools.partial(_ab_kernel, seq // tm),
        grid=(t // tm,),
        in_specs=[
            pl.BlockSpec((tm, d), lambda i: (i, 0)),
            _resident(wa.shape), _resident(wb.shape),
            _resident(a_conv_w.shape), _resident(b_conv_w.shape),
            _resident(b_conv_b.shape), _resident(b_ln_g.shape), _resident(b_ln_b.shape),
        ],
        out_specs=pl.BlockSpec((tm, A_WIDTH + B_WIDTH), lambda i: (i, 0)),
        out_shape=jax.ShapeDtypeStruct((t, A_WIDTH + B_WIDTH), BF16),
        scratch_shapes=[
            pltpu.VMEM((A_HIST + tm, A_WIDTH), F32),
            pltpu.VMEM((B_HIST + tm, B_WIDTH), F32),
            pltpu.VMEM((SUBLANES - 1, B_HIST + tm - SUBLANES, B_WIDTH), F32),
        ],
        compiler_params=_params(1),
        name="conv_branches",
    )(u, wa, wb, a_conv_w, b_conv_w, b_conv_b, b_ln_g, b_ln_b)


def _scores(kaug_ref, qx_ref, j):
    w = (j + 1) * Q_BLOCK
    return lax.dot_general(kaug_ref[0:w, :], qx_ref[j], (((1,), (1,)), ((), ())),
                           preferred_element_type=F32)


def _weighted_values(s, vt_ref, mask_bias, j):
    q0 = j * Q_BLOCK
    w = q0 + Q_BLOCK
    s_diag = s[q0:w, :] + mask_bias
    m = jnp.max(s_diag, axis=0, keepdims=True)
    if j > 0:
        s_top = s[0:q0, :]
        m = jnp.maximum(m, jnp.max(s_top, axis=0, keepdims=True))
        p = jnp.concatenate([jnp.exp2(s_top - m), jnp.exp2(s_diag - m)], axis=0)
    else:
        p = jnp.exp2(s_diag - m)
    return jnp.dot(vt_ref[:, 0:w], p.astype(BF16), preferred_element_type=F32)


def _attend_all(kaug_ref, qx_ref, vt_ref, n_blocks, emit):
    key = lax.broadcasted_iota(jnp.int32, (Q_BLOCK, 2 * Q_BLOCK), 0)
    qry = lax.broadcasted_iota(jnp.int32, (Q_BLOCK, 2 * Q_BLOCK), 1) % Q_BLOCK
    mask_bias = jnp.where(key <= qry, 0.0, NEG_BIG).astype(F32)
    s_next = _scores(kaug_ref, qx_ref, 0)
    for j in range(n_blocks):
        s = s_next
        if j + 1 < n_blocks:
            s_next = _scores(kaug_ref, qx_ref, j + 1)
        o = _weighted_values(s, vt_ref, mask_bias, j)
        o0 = o[0:C_V_DIM, 0:Q_BLOCK] / o[C_V_DIM:C_V_DIM + 1, 0:Q_BLOCK]
        o1 = o[0:C_V_DIM, Q_BLOCK:2 * Q_BLOCK] / o[C_V_DIM:C_V_DIM + 1, Q_BLOCK:2 * Q_BLOCK]
        emit(j, o0, o1)


def _store_queries(qx_ref, q, split):
    n_blocks = qx_ref.shape[0]
    lane = lax.broadcasted_iota(jnp.int32, q.shape, 1)
    qx_ref[:, 0:Q_BLOCK, 0:LANES] = (
        jnp.where(lane < split, q, 0.0).astype(BF16).reshape(n_blocks, Q_BLOCK, LANES))
    qx_ref[:, Q_BLOCK:2 * Q_BLOCK, 0:LANES] = (
        jnp.where(lane >= split, q, 0.0).astype(BF16).reshape(n_blocks, Q_BLOCK, LANES))


def _init_vt_tail(vt_ref):
    seq = vt_ref.shape[1]
    row = lax.broadcasted_iota(jnp.int32, (V_ROWS - C_V_DIM, seq), 0)
    vt_ref[C_V_DIM:V_ROWS, :] = jnp.where(row == 0, 1.0, 0.0).astype(BF16)


def _store_bias_selectors(qx_ref, first_lane0, first_lane1):
    n_blocks = qx_ref.shape[0]
    lane = lax.broadcasted_iota(jnp.int32, (n_blocks, Q_BLOCK, LANES), 2)
    for half, first in enumerate((first_lane0, first_lane1)):
        hit = jnp.logical_and(lane >= first, lane < first + BIAS_PIECES)
        qx_ref[:, half * Q_BLOCK:(half + 1) * Q_BLOCK, LANES:2 * LANES] = (
            jnp.where(hit, 1.0, 0.0).astype(BF16))


def _attn_c_kernel(lambda_init, u_ref, w_ref, slope_ref, lam_ref, subg_ref, o_ref,
                   kaug, qx, vt, ot):
    seq = u_ref.shape[0]
    b = pl.program_id(0)
    head = pl.program_id(1)

    @pl.when(jnp.logical_and(b == 0, head == 0))
    def _():
        _init_vt_tail(vt)
        pos = lax.broadcasted_iota(jnp.int32, (seq, LANES), 0).astype(F32)
        lane = lax.broadcasted_iota(jnp.int32, (seq, LANES), 1)
        kaug[:, LANES:2 * LANES] = _piece_slab(pos * (slope_ref[...] * LOG2_E), lane)

    qkv = jnp.dot(u_ref[...], w_ref[0], preferred_element_type=F32)
    _store_queries(qx, qkv[:, 0:LANES] * (C_QK_DIM ** -0.5 * LOG2_E), C_QK_DIM)
    _store_bias_selectors(qx, BIAS_PIECES * head, BIAS_PIECES * head)
    kaug[:, 0:LANES] = qkv[:, LANES:2 * LANES].astype(BF16)
    vt[0:C_V_DIM, :] = qkv[:, 2 * LANES:3 * LANES].T.astype(BF16)

    lp = lam_ref[...]
    lam = (jnp.exp(jnp.sum(lp[0:1, :] * lp[1:2, :], keepdims=True))
           - jnp.exp(jnp.sum(lp[2:3, :] * lp[3:4, :], keepdims=True)) + lambda_init)

    def emit(j, o0, o1):
        ot[:, j * Q_BLOCK:(j + 1) * Q_BLOCK] = o0 - lam * o1

    _attend_all(kaug, qx, vt, seq // Q_BLOCK, emit)
    o = ot[...].T
    o_ref[...] = (_rms(o, subg_ref[...]) * (1.0 - lambda_init)).astype(BF16)


def _attn_d_kernel(u_ref, w_ref, fb_ref, o_ref, kaug, qx, vt, ot):
    seq = u_ref.shape[0]
    b = pl.program_id(0)
    pair = pl.program_id(1)

    @pl.when(jnp.logical_and(b == 0, pair == 0))
    def _():
        _init_vt_tail(vt)

    qkv = jnp.dot(u_ref[...], w_ref[0], preferred_element_type=F32)

    @pl.when(pair == 0)
    def _():
        x = qkv[:, 3 * LANES:4 * LANES] + fb_ref[...]
        logf = jnp.minimum(x, 0.0) - jnp.log1p(jnp.exp(-jnp.abs(x)))
        row = lax.broadcasted_iota(jnp.int32, (seq, LANES), 0)
        c = logf
        shift = 1
        while shift < seq:
            c = c + jnp.where(row >= shift, pltpu.roll(c, shift, axis=0), 0.0)
            shift *= 2
        lane = lax.broadcasted_iota(jnp.int32, (seq, LANES), 1)
        kaug[:, LANES:2 * LANES] = _piece_slab(c * (-LOG2_E), lane)

    _store_queries(qx, qkv[:, 0:LANES] * (D_HEAD_DIM ** -0.5 * LOG2_E), D_HEAD_DIM)
    _store_bias_selectors(qx, BIAS_PIECES * 2 * pair, BIAS_PIECES * (2 * pair + 1))
    kaug[:, 0:LANES] = qkv[:, LANES:2 * LANES].astype(BF16)
    vt[0:C_V_DIM, :] = qkv[:, 2 * LANES:3 * LANES].T.astype(BF16)

    vrow = lax.broadcasted_iota(jnp.int32, (C_V_DIM, Q_BLOCK), 0)

    def emit(j, o0, o1):
        ot[:, j * Q_BLOCK:(j + 1) * Q_BLOCK] = jnp.where(vrow < D_HEAD_DIM, o0, o1)

    _attend_all(kaug, qx, vt, seq // Q_BLOCK, emit)
    o_ref[...] = ot[...].T.astype(BF16)


def _attn_scratch(seq):
    return [
        pltpu.VMEM((seq, 2 * LANES), BF16),
        pltpu.VMEM((seq // Q_BLOCK, 2 * Q_BLOCK, 2 * LANES), BF16),
        pltpu.VMEM((V_ROWS, seq), BF16),
        pltpu.VMEM((C_V_DIM, seq), F32),
    ]


def _attn_c(u, seq, w, slope_vec, lam_params, subln_g, lambda_init):
    t, d = u.shape
    return pl.pallas_call(
        functools.partial(_attn_c_kernel, lambda_init),
        grid=(t // seq, C_HEADS),
        in_specs=[
            pl.BlockSpec((seq, d), lambda b, h: (b, 0)),
            pl.BlockSpec((1,) + w.shape[1:], lambda b, h: (h, 0, 0)),
            _resident(slope_vec.shape), _resident(lam_params.shape), _resident(subln_g.shape),
        ],
        out_specs=pl.BlockSpec((seq, LANES), lambda b, h: (b, h)),
        out_shape=jax.ShapeDtypeStruct((t, C_HEADS * C_V_DIM), BF16),
        scratch_shapes=_attn_scratch(seq),
        compiler_params=_params(2),
        name="diff_attention",
    )(u, w, slope_vec, lam_params, subln_g)


def _attn_d(u, seq, w, fbias_vec):
    t, d = u.shape
    n_pairs = D_HEADS // 2
    return pl.pallas_call(
        _attn_d_kernel,
        grid=(t // seq, n_pairs),
        in_specs=[
            pl.BlockSpec((seq, d), lambda b, p: (b, 0)),
            pl.BlockSpec((1,) + w.shape[1:], lambda b, p: (p, 0, 0)),
            _resident(fbias_vec.shape),
        ],
        out_specs=pl.BlockSpec((seq, LANES), lambda b, p: (b, p)),
        out_shape=jax.ShapeDtypeStruct((t, D_HEADS * D_HEAD_DIM), BF16),
        scratch_shapes=_attn_scratch(seq),
        compiler_params=_params(2),
        name="forget_attention",
    )(u, w, fbias_vec)


def _merge_kernel(u_ref, h_ref, oab_ref, oc_ref, od_ref, wg_ref, wa_ref, wb_ref, wc_ref, wd_ref,
                  wo_ref, post_g_ref, out_ref):
    d = h_ref.shape[1]
    u = u_ref[...]
    branches = (
        (oab_ref[:, 0:A_WIDTH], wa_ref),
        (oab_ref[:, A_WIDTH:A_WIDTH + B_WIDTH], wb_ref),
        (oc_ref[...], wc_ref),
        (od_ref[...], wd_ref),
    )
    merged = None
    for i, (o, w_ref) in enumerate(branches):
        gate = jax.nn.sigmoid(jnp.dot(u, wg_ref[:, i * d:(i + 1) * d], preferred_element_type=F32))
        term = gate * jnp.dot(o, w_ref[...], preferred_element_type=F32)
        merged = term if merged is None else merged + term
    y = jnp.dot(merged.astype(BF16), wo_ref[...], preferred_element_type=F32)
    out_ref[...] = h_ref[...] + _rms(y, post_g_ref[...])


def _merge(u, h, oab, oc, od, wg, wa, wb, wc, wd, wo, post_g):
    t, d = h.shape
    tm = min(TOKEN_TILE, t)

    def row(width):
        return pl.BlockSpec((tm, width), lambda i: (i, 0))

    return pl.pallas_call(
        _merge_kernel,
        grid=(t // tm,),
        in_specs=[
            row(d), row(d), row(oab.shape[1]), row(oc.shape[1]), row(od.shape[1]),
            _resident(wg.shape), _resident(wa.shape), _resident(wb.shape), _resident(wc.shape),
            _resident(wd.shape), _resident(wo.shape), _resident(post_g.shape),
        ],
        out_specs=row(d),
        out_shape=jax.ShapeDtypeStruct((t, d), F32),
        compiler_params=_params(1),
        name="gated_merge",
    )(u, h, oab, oc, od, wg, wa, wb, wc, wd, wo, post_g)


def _column_offsets(d_model):
    sizes = (A_WIDTH, A_WIDTH, A_WIDTH, 2 * B_WIDTH,
             C_HEADS * 2 * C_QK_DIM, C_HEADS * 2 * C_QK_DIM, C_HEADS * C_V_DIM,
             D_HEADS * D_HEAD_DIM, D_HEADS * D_HEAD_DIM, D_HEADS * D_HEAD_DIM, D_HEADS,
             N_BRANCHES * d_model)
    offs = [0]
    for s in sizes:
        offs.append(offs[-1] + s)
    return offs


def _per_group(w, n_groups):
    d = w.shape[0]
    return w.reshape(d, n_groups, LANES).transpose(1, 0, 2)


def _replicated_lanes(vals):
    rep = jnp.repeat(vals, BIAS_PIECES, axis=-1)
    pad = [(0, 0)] * (rep.ndim - 1) + [(0, LANES - rep.shape[-1])]
    return jnp.pad(rep, pad)


def kernel(x, ffn1_pre_g, ffn1_post_g, ffn1_w_gate, ffn1_w_up, ffn1_w_down, mix_pre_g, mix_post_g, w_in, a_conv_w, a_w_out, b_conv_w, b_conv_b, b_ln_g, b_ln_b, b_w_out, c_lam_q1, c_lam_k1, c_lam_q2, c_lam_k2, c_subln_g, c_w_out, d_forget_b, d_w_out, w_o, ffn2_pre_g, ffn2_post_g, ffn2_w_gate, ffn2_w_up, ffn2_w_down):
    bsz, seq, d = x.shape
    depth = w_in.shape[0]
    assert seq % Q_BLOCK == 0 and seq % min(TOKEN_TILE, seq) == 0
    offs = _column_offsets(d)
    h = x.reshape(bsz * seq, d)

    def vec(v):
        return v.reshape(1, -1).astype(F32)

    slope_vec = _replicated_lanes(
        jnp.exp2(-8.0 / C_HEADS * jnp.arange(1, C_HEADS + 1, dtype=F32)))[None, :]

    for l in range(depth):
        lambda_init = 0.8 - 0.6 * math.exp(-0.3 * l)
        wl = w_in[l].astype(BF16)

        def cols(i, j=None):
            return wl[:, offs[i]:offs[i + 1 if j is None else j]]

        w_a = cols(0, 3)
        w_b = cols(3)
        w_c = jnp.concatenate([_per_group(cols(4), C_HEADS), _per_group(cols(5), C_HEADS),
                               _per_group(cols(6), C_HEADS)], axis=-1)
        n_pairs = D_HEADS // 2
        f_slab = jnp.broadcast_to(_replicated_lanes(cols(10))[None], (n_pairs, d, LANES))
        w_d = jnp.concatenate([_per_group(cols(7), n_pairs), _per_group(cols(8), n_pairs),
                               _per_group(cols(9), n_pairs), f_slab], axis=-1)
        w_g = cols(11)

        h, u = _ffn(h, vec(ffn1_pre_g[l]), vec(ffn1_post_g[l]), ffn1_w_gate[l].astype(BF16),
                    ffn1_w_up[l].astype(BF16), ffn1_w_down[l].astype(BF16), vec(mix_pre_g[l]))
        oab = _ab(u, seq, w_a, w_b, a_conv_w[l], b_conv_w[l], vec(b_conv_b[l]), vec(b_ln_g[l]),
                  vec(b_ln_b[l]))
        lam_params = jnp.stack([c_lam_q1[l], c_lam_k1[l], c_lam_q2[l], c_lam_k2[l]]).astype(F32)
        oc = _attn_c(u, seq, w_c, slope_vec, lam_params, vec(c_subln_g[l]), lambda_init)
        od = _attn_d(u, seq, w_d, _replicated_lanes(d_forget_b[l].astype(F32))[None, :])
        h = _merge(u, h, oab, oc, od, w_g, a_w_out[l].astype(BF16), b_w_out[l].astype(BF16),
                   c_w_out[l].astype(BF16), d_w_out[l].astype(BF16), w_o[l].astype(BF16),
                   vec(mix_post_g[l]))
        h = _ffn(h, vec(ffn2_pre_g[l]), vec(ffn2_post_g[l]), ffn2_w_gate[l].astype(BF16),
                 ffn2_w_up[l].astype(BF16), ffn2_w_down[l].astype(BF16))
    return h.reshape(bsz, seq, d)
```

```python
import functools
import math

import jax
import jax.numpy as jnp
from jax import lax
from jax.experimental import pallas as pl
from jax.experimental.pallas import tpu as pltpu

F32 = jnp.float32
BF16 = jnp.bfloat16

EPS = 1e-6
FFN_RESIDUAL_SCALE = 0.5
D_FF = 2816
A_WIDTH = 512
A_CONV = 3
B_WIDTH = 512
B_CONV = 31
C_HEADS = 4
C_QK_DIM = 64
C_V_DIM = 128
D_HEADS = 8
D_HEAD_DIM = 64
N_BRANCHES = 4

LANES = 128
SUBLANES = 8
MXU_DIM = 256
VMEM_LIMIT_BYTES = 56 * 1024 * 1024

TOKEN_TILE = 512
FFN_TOKEN_TILE = 1024
SUB_ROWS = 256
Q_BLOCK = MXU_DIM
SCORE_LOOKAHEAD = 2
BIAS_PIECES = 3
A_HIST = SUBLANES
B_HIST = 32
CONV_ROWS = 64
V_ROWS = C_V_DIM + 16
NEG_BIG = -1e30
LOG2_E = math.log2(math.e)


def _rms(x, g):
    return x * lax.rsqrt(jnp.mean(x * x, axis=-1, keepdims=True) + EPS) * g


def _split3(x):
    hi = x.astype(BF16).astype(F32)
    r = x - hi
    mid = r.astype(BF16).astype(F32)
    lo = (r - mid).astype(BF16).astype(F32)
    return hi, mid, lo


def _piece_slab(x, lane):
    hi, mid, lo = _split3(x)
    r = lane % BIAS_PIECES
    return jnp.where(r == 0, hi, jnp.where(r == 1, mid, lo)).astype(BF16)


def _resident(shape):
    nd = len(shape)
    return pl.BlockSpec(shape, lambda *_: (0,) * nd, pipeline_mode=pl.Buffered(1))


def _params(n_axes):
    return pltpu.CompilerParams(
        dimension_semantics=("arbitrary",) * n_axes,
        vmem_limit_bytes=VMEM_LIMIT_BYTES,
    )


def _ffn_kernel(emit_norm, x_ref, pre_g_ref, post_g_ref, wg_ref, wu_ref, wd_ref, *rest):
    if emit_norm:
        next_g_ref, h_ref, u_ref = rest
    else:
        (h_ref,) = rest
    subs = [slice(r, r + SUB_ROWS) for r in range(0, x_ref.shape[0], SUB_ROWS)]
    xs = [x_ref[sl, :] for sl in subs]
    xns = [_rms(x, pre_g_ref[...]).astype(BF16) for x in xs]
    gs, us = [], []
    for xn in xns:
        gs.append(jnp.dot(xn, wg_ref[...], preferred_element_type=F32))
        us.append(jnp.dot(xn, wu_ref[...], preferred_element_type=F32))
    fs = []
    for g, u in zip(gs, us):
        a = (g * jax.nn.sigmoid(g) * u).astype(BF16)
        fs.append(jnp.dot(a, wd_ref[...], preferred_element_type=F32))
    for sl, x, f in zip(subs, xs, fs):
        h = x + FFN_RESIDUAL_SCALE * _rms(f, post_g_ref[...])
        h_ref[sl, :] = h
        if emit_norm:
            u_ref[sl, :] = _rms(h, next_g_ref[...]).astype(BF16)


def _ffn(x, pre_g, post_g, wg, wu, wd, next_g=None):
    t, d = x.shape
    tm = min(FFN_TOKEN_TILE, t)
    emit_norm = next_g is not None
    row = pl.BlockSpec((tm, d), lambda i: (i, 0))
    vec = _resident((1, d))
    in_specs = [row, vec, vec, _resident(wg.shape), _resident(wu.shape), _resident(wd.shape)]
    args = [x, pre_g, post_g, wg, wu, wd]
    out_shape = [jax.ShapeDtypeStruct((t, d), F32)]
    out_specs = [row]
    if emit_norm:
        in_specs.append(vec)
        args.append(next_g)
        out_shape.append(jax.ShapeDtypeStruct((t, d), BF16))
        out_specs.append(row)
    out = pl.pallas_call(
        functools.partial(_ffn_kernel, emit_norm),
        grid=(t // tm,),
        in_specs=in_specs,
        out_specs=out_specs,
        out_shape=out_shape,
        compiler_params=_params(1),
        name="ffn_norm" if emit_norm else "ffn",
    )(*args)
    return out if emit_norm else out[0]


def _causal_conv(buf_ref, hist, w, n_taps, tm):
    chunks = []
    for r0 in range(0, tm, CONV_ROWS):
        acc = None
        for k in range(n_taps):
            start = hist + r0 - (n_taps - 1) + k
            term = buf_ref[start:start + CONV_ROWS, :] * w[k:k + 1, :]
            acc = term if acc is None else acc + term
        chunks.append(acc)
    return jnp.concatenate(chunks, axis=0)


def _causal_conv_aligned(buf_ref, shift_ref, hist, w, n_taps, tm):
    rows = shift_ref.shape[1]
    filled = 0
    chunks = []
    for r0 in range(0, tm, CONV_ROWS):
        need = min(rows, r0 + CONV_ROWS + hist - SUBLANES)
        if need > filled:
            for r in range(1, SUBLANES):
                shift_ref[r - 1, filled:need, :] = buf_ref[filled + r:need + r, :]
            filled = need
        acc = None
        for k in range(n_taps):
            base = hist - (n_taps - 1) + k
            phase = base % SUBLANES
            start = base - phase + r0
            if phase == 0:
                x = buf_ref[start:start + CONV_ROWS, :]
            else:
                x = shift_ref[phase - 1, start:start + CONV_ROWS, :]
            term = x * w[k:k + 1, :]
            acc = term if acc is None else acc + term
        chunks.append(acc)
    return jnp.concatenate(chunks, axis=0)


def _ab_kernel(tiles_per_seq, u_ref, wa_ref, wb_ref, aw_ref, bw_ref, bb_ref, lng_ref, lnb_ref,
               o_ref, zbuf, ybuf, yshift):
    tm = u_ref.shape[0]
    first = (pl.program_id(0) % tiles_per_seq) == 0

    @pl.when(first)
    def _():
        zbuf[0:A_HIST, :] = jnp.zeros((A_HIST, A_WIDTH), F32)
        ybuf[0:B_HIST, :] = jnp.zeros((B_HIST, B_WIDTH), F32)

    @pl.when(jnp.logical_not(first))
    def _():
        zbuf[0:A_HIST, :] = zbuf[tm:tm + A_HIST, :]
        ybuf[0:B_HIST, :] = ybuf[tm:tm + B_HIST, :]

    half = tm // 2
    for r0 in (0, half):
        pb = jnp.dot(u_ref[r0:r0 + half, :], wb_ref[...], preferred_element_type=F32)
        ybuf[B_HIST + r0:B_HIST + r0 + half, :] = (
            pb[:, 0:B_WIDTH] * jax.nn.sigmoid(pb[:, B_WIDTH:2 * B_WIDTH]))
    pa = jnp.dot(u_ref[...], wa_ref[...], preferred_element_type=F32)

    y = _causal_conv_aligned(ybuf, yshift, B_HIST, bw_ref[...], B_CONV, tm) + bb_ref[...]
    mu = jnp.mean(y, axis=-1, keepdims=True)
    yc = y - mu
    var = jnp.mean(yc * yc, axis=-1, keepdims=True)
    yn = yc * lax.rsqrt(var + EPS) * lng_ref[...] + lnb_ref[...]
    o_b = yn * jax.nn.sigmoid(yn)
    o_ref[:, A_WIDTH:A_WIDTH + B_WIDTH] = o_b.astype(BF16)

    gate_b = pa[:, 0:A_WIDTH]
    zbuf[A_HIST:A_HIST + tm, :] = pa[:, A_WIDTH:2 * A_WIDTH] * pa[:, 2 * A_WIDTH:3 * A_WIDTH]
    o_a = gate_b * _causal_conv(zbuf, A_HIST, aw_ref[...], A_CONV, tm)
    o_ref[:, 0:A_WIDTH] = o_a.astype(BF16)


def _ab(u, seq, wa, wb, a_conv_w, b_conv_w, b_conv_b, b_ln_g, b_ln_b):
    t, d = u.shape
    tm = min(TOKEN_TILE, seq)
    return pl.pallas_call(
        functools.partial(_ab_kernel, seq // tm),
        grid=(t // tm,),
        in_specs=[
            pl.BlockSpec((tm, d), lambda i: (i, 0)),
            _resident(wa.shape), _resident(wb.shape),
            _resident(a_conv_w.shape), _resident(b_conv_w.shape),
            _resident(b_conv_b.shape), _resident(b_ln_g.shape), _resident(b_ln_b.shape),
        ],
        out_specs=pl.BlockSpec((tm, A_WIDTH + B_WIDTH), lambda i: (i, 0)),
        out_shape=jax.ShapeDtypeStruct((t, A_WIDTH + B_WIDTH), BF16),
        scratch_shapes=[
            pltpu.VMEM((A_HIST + tm, A_WIDTH), F32),
            pltpu.VMEM((B_HIST + tm, B_WIDTH), F32),
            pltpu.VMEM((SUBLANES - 1, B_HIST + tm - SUBLANES, B_WIDTH), F32),
        ],
        compiler_params=_params(1),
        name="conv_branches",
    )(u, wa, wb, a_conv_w, b_conv_w, b_conv_b, b_ln_g, b_ln_b)


def _scores(kaug_ref, qx_ref, j):
    w = (j + 1) * Q_BLOCK
    return lax.dot_general(kaug_ref[0:w, :], qx_ref[j], (((1,), (1,)), ((), ())),
                           preferred_element_type=F32)


def _weighted_values(s, vt_ref, mask_bias, j):
    q0 = j * Q_BLOCK
    w = q0 + Q_BLOCK
    s_diag = s[q0:w, :] + mask_bias
    m = jnp.max(s_diag, axis=0, keepdims=True)
    if j > 0:
        s_top = s[0:q0, :]
        m = jnp.maximum(m, jnp.max(s_top, axis=0, keepdims=True))
        p = jnp.concatenate([jnp.exp2(s_top - m), jnp.exp2(s_diag - m)], axis=0)
    else:
        p = jnp.exp2(s_diag - m)
    return jnp.dot(vt_ref[:, 0:w], p.astype(BF16), preferred_element_type=F32)


def _attend_all(kaug_ref, qx_ref, vt_ref, n_blocks, emit):
    key = lax.broadcasted_iota(jnp.int32, (Q_BLOCK, 2 * Q_BLOCK), 0)
    qry = lax.broadcasted_iota(jnp.int32, (Q_BLOCK, 2 * Q_BLOCK), 1) % Q_BLOCK
    mask_bias = jnp.where(key <= qry, 0.0, NEG_BIG).astype(F32)
    pending = [_scores(kaug_ref, qx_ref, j) for j in range(min(SCORE_LOOKAHEAD, n_blocks))]
    for j in range(n_blocks):
        s = pending.pop(0)
        if j + SCORE_LOOKAHEAD < n_blocks:
            pending.append(_scores(kaug_ref, qx_ref, j + SCORE_LOOKAHEAD))
        o = _weighted_values(s, vt_ref, mask_bias, j)
        o0 = o[0:C_V_DIM, 0:Q_BLOCK] / o[C_V_DIM:C_V_DIM + 1, 0:Q_BLOCK]
        o1 = o[0:C_V_DIM, Q_BLOCK:2 * Q_BLOCK] / o[C_V_DIM:C_V_DIM + 1, Q_BLOCK:2 * Q_BLOCK]
        emit(j, o0, o1)


def _project(u_ref, w, kaug, qx, vt, q_scale, split, extra=None):
    seq = u_ref.shape[0]
    half_rows = seq // 2
    extras = []
    for r0 in (0, half_rows):
        qkv = jnp.dot(u_ref[r0:r0 + half_rows, :], w, preferred_element_type=F32)
        q = qkv[:, 0:LANES] * q_scale
        lane = lax.broadcasted_iota(jnp.int32, q.shape, 1)
        b0, nb = r0 // Q_BLOCK, half_rows // Q_BLOCK
        qx[b0:b0 + nb, 0:Q_BLOCK, 0:LANES] = (
            jnp.where(lane < split, q, 0.0).astype(BF16).reshape(nb, Q_BLOCK, LANES))
        qx[b0:b0 + nb, Q_BLOCK:2 * Q_BLOCK, 0:LANES] = (
            jnp.where(lane >= split, q, 0.0).astype(BF16).reshape(nb, Q_BLOCK, LANES))
        kaug[r0:r0 + half_rows, 0:LANES] = qkv[:, LANES:2 * LANES].astype(BF16)
        vt[0:C_V_DIM, r0:r0 + half_rows] = qkv[:, 2 * LANES:3 * LANES].T.astype(BF16)
        if extra is not None:
            extras.append(qkv[:, extra])
    return extras


def _init_vt_tail(vt_ref):
    seq = vt_ref.shape[1]
    row = lax.broadcasted_iota(jnp.int32, (V_ROWS - C_V_DIM, seq), 0)
    vt_ref[C_V_DIM:V_ROWS, :] = jnp.where(row == 0, 1.0, 0.0).astype(BF16)


def _store_bias_selectors(qx_ref, first_lane0, first_lane1):
    lane = lax.broadcasted_iota(jnp.int32, (1, LANES), 1)
    for half, first in enumerate((first_lane0, first_lane1)):
        hit = jnp.logical_and(lane >= first, lane < first + BIAS_PIECES)
        slab = jnp.broadcast_to(jnp.where(hit, 1.0, 0.0), (Q_BLOCK, LANES)).astype(BF16)
        for j in range(qx_ref.shape[0]):
            qx_ref[j, half * Q_BLOCK:(half + 1) * Q_BLOCK, LANES:2 * LANES] = slab


def _attn_c_kernel(lambda_init, u_ref, w_ref, slope_ref, lam_ref, subg_ref, o_ref,
                   kaug, qx, vt):
    seq = u_ref.shape[0]
    b = pl.program_id(0)
    head = pl.program_id(1)

    @pl.when(jnp.logical_and(b == 0, head == 0))
    def _():
        _init_vt_tail(vt)
        pos = lax.broadcasted_iota(jnp.int32, (seq, LANES), 0).astype(F32)
        lane = lax.broadcasted_iota(jnp.int32, (seq, LANES), 1)
        kaug[:, LANES:2 * LANES] = _piece_slab(pos * (slope_ref[...] * LOG2_E), lane)

    _store_bias_selectors(qx, BIAS_PIECES * head, BIAS_PIECES * head)
    _project(u_ref, w_ref[0], kaug, qx, vt, C_QK_DIM ** -0.5 * LOG2_E, C_QK_DIM)

    lp = lam_ref[...]
    lam = (jnp.exp(jnp.sum(lp[0:1, :] * lp[1:2, :], keepdims=True))
           - jnp.exp(jnp.sum(lp[2:3, :] * lp[3:4, :], keepdims=True)) + lambda_init)

    def emit(j, o0, o1):
        o = (o0 - lam * o1).T
        o_ref[j * Q_BLOCK:(j + 1) * Q_BLOCK, :] = (
            _rms(o, subg_ref[...]) * (1.0 - lambda_init)).astype(BF16)

    _attend_all(kaug, qx, vt, seq // Q_BLOCK, emit)


def _attn_d_kernel(u_ref, w_ref, fb_ref, o_ref, kaug, qx, vt):
    seq = u_ref.shape[0]
    b = pl.program_id(0)
    pair = pl.program_id(1)

    @pl.when(jnp.logical_and(b == 0, pair == 0))
    def _():
        _init_vt_tail(vt)

    _store_bias_selectors(qx, BIAS_PIECES * 2 * pair, BIAS_PIECES * (2 * pair + 1))
    f_logits = _project(u_ref, w_ref[0], kaug, qx, vt, D_HEAD_DIM ** -0.5 * LOG2_E, D_HEAD_DIM,
                        extra=slice(3 * LANES, 4 * LANES))

    @pl.when(pair == 0)
    def _():
        x = jnp.concatenate(f_logits, axis=0) + fb_ref[...]
        logf = jnp.minimum(x, 0.0) - jnp.log1p(jnp.exp(-jnp.abs(x)))
        row = lax.broadcasted_iota(jnp.int32, (seq, LANES), 0)
        c = logf
        shift = 1
        while shift < seq:
            c = c + jnp.where(row >= shift, pltpu.roll(c, shift, axis=0), 0.0)
            shift *= 2
        lane = lax.broadcasted_iota(jnp.int32, (seq, LANES), 1)
        kaug[:, LANES:2 * LANES] = _piece_slab(c * (-LOG2_E), lane)

    vrow = lax.broadcasted_iota(jnp.int32, (C_V_DIM, Q_BLOCK), 0)

    def emit(j, o0, o1):
        o_ref[j * Q_BLOCK:(j + 1) * Q_BLOCK, :] = (
            jnp.where(vrow < D_HEAD_DIM, o0, o1).T.astype(BF16))

    _attend_all(kaug, qx, vt, seq // Q_BLOCK, emit)


def _attn_scratch(seq):
    return [
        pltpu.VMEM((seq, 2 * LANES), BF16),
        pltpu.VMEM((seq // Q_BLOCK, 2 * Q_BLOCK, 2 * LANES), BF16),
        pltpu.VMEM((V_ROWS, seq), BF16),
    ]


def _attn_c(u, seq, w, slope_vec, lam_params, subln_g, lambda_init):
    t, d = u.shape
    return pl.pallas_call(
        functools.partial(_attn_c_kernel, lambda_init),
        grid=(t // seq, C_HEADS),
        in_specs=[
            pl.BlockSpec((seq, d), lambda b, h: (b, 0)),
            pl.BlockSpec((1,) + w.shape[1:], lambda b, h: (h, 0, 0)),
            _resident(slope_vec.shape), _resident(lam_params.shape), _resident(subln_g.shape),
        ],
        out_specs=pl.BlockSpec((seq, LANES), lambda b, h: (b, h)),
        out_shape=jax.ShapeDtypeStruct((t, C_HEADS * C_V_DIM), BF16),
        scratch_shapes=_attn_scratch(seq),
        compiler_params=_params(2),
        name="diff_attention",
    )(u, w, slope_vec, lam_params, subln_g)


def _attn_d(u, seq, w, fbias_vec):
    t, d = u.shape
    n_pairs = D_HEADS // 2
    return pl.pallas_call(
        _attn_d_kernel,
        grid=(t // seq, n_pairs),
        in_specs=[
            pl.BlockSpec((seq, d), lambda b, p: (b, 0)),
            pl.BlockSpec((1,) + w.shape[1:], lambda b, p: (p, 0, 0)),
            _resident(fbias_vec.shape),
        ],
        out_specs=pl.BlockSpec((seq, LANES), lambda b, p: (b, p)),
        out_shape=jax.ShapeDtypeStruct((t, D_HEADS * D_HEAD_DIM), BF16),
        scratch_shapes=_attn_scratch(seq),
        compiler_params=_params(2),
        name="forget_attention",
    )(u, w, fbias_vec)


def _merge_kernel(u_ref, h_ref, oab_ref, oc_ref, od_ref, wg_ref, wa_ref, wb_ref, wc_ref, wd_ref,
                  wo_ref, post_g_ref, out_ref):
    d = h_ref.shape[1]
    wos = (wa_ref, wb_ref, wc_ref, wd_ref)
    subs = [slice(r, r + SUB_ROWS) for r in range(0, h_ref.shape[0], SUB_ROWS)]
    merged = []
    for sl in subs:
        u = u_ref[sl, :]
        outs = (oab_ref[sl, 0:A_WIDTH], oab_ref[sl, A_WIDTH:A_WIDTH + B_WIDTH], oc_ref[sl, :],
                od_ref[sl, :])
        acc = None
        for i, (o, w_ref) in enumerate(zip(outs, wos)):
            gate = jax.nn.sigmoid(
                jnp.dot(u, wg_ref[:, i * d:(i + 1) * d], preferred_element_type=F32))
            term = gate * jnp.dot(o, w_ref[...], preferred_element_type=F32)
            acc = term if acc is None else acc + term
        merged.append(acc.astype(BF16))
    ys = [jnp.dot(m, wo_ref[...], preferred_element_type=F32) for m in merged]
    for sl, y in zip(subs, ys):
        out_ref[sl, :] = h_ref[sl, :] + _rms(y, post_g_ref[...])


def _merge(u, h, oab, oc, od, wg, wa, wb, wc, wd, wo, post_g):
    t, d = h.shape
    tm = min(TOKEN_TILE, t)

    def row(width):
        return pl.BlockSpec((tm, width), lambda i: (i, 0))

    return pl.pallas_call(
        _merge_kernel,
        grid=(t // tm,),
        in_specs=[
            row(d), row(d), row(oab.shape[1]), row(oc.shape[1]), row(od.shape[1]),
            _resident(wg.shape), _resident(wa.shape), _resident(wb.shape), _resident(wc.shape),
            _resident(wd.shape), _resident(wo.shape), _resident(post_g.shape),
        ],
        out_specs=row(d),
        out_shape=jax.ShapeDtypeStruct((t, d), F32),
        compiler_params=_params(1),
        name="gated_merge",
    )(u, h, oab, oc, od, wg, wa, wb, wc, wd, wo, post_g)


def _column_offsets(d_model):
    sizes = (A_WIDTH, A_WIDTH, A_WIDTH, 2 * B_WIDTH,
             C_HEADS * 2 * C_QK_DIM, C_HEADS * 2 * C_QK_DIM, C_HEADS * C_V_DIM,
             D_HEADS * D_HEAD_DIM, D_HEADS * D_HEAD_DIM, D_HEADS * D_HEAD_DIM, D_HEADS,
             N_BRANCHES * d_model)
    offs = [0]
    for s in sizes:
        offs.append(offs[-1] + s)
    return offs


def _per_group(w, n_groups):
    d = w.shape[0]
    return w.reshape(d, n_groups, LANES).transpose(1, 0, 2)


def _replicated_lanes(vals):
    rep = jnp.repeat(vals, BIAS_PIECES, axis=-1)
    pad = [(0, 0)] * (rep.ndim - 1) + [(0, LANES - rep.shape[-1])]
    return jnp.pad(rep, pad)


def kernel(x, ffn1_pre_g, ffn1_post_g, ffn1_w_gate, ffn1_w_up, ffn1_w_down, mix_pre_g, mix_post_g, w_in, a_conv_w, a_w_out, b_conv_w, b_conv_b, b_ln_g, b_ln_b, b_w_out, c_lam_q1, c_lam_k1, c_lam_q2, c_lam_k2, c_subln_g, c_w_out, d_forget_b, d_w_out, w_o, ffn2_pre_g, ffn2_post_g, ffn2_w_gate, ffn2_w_up, ffn2_w_down):
    bsz, seq, d = x.shape
    depth = w_in.shape[0]
    assert seq % (2 * Q_BLOCK) == 0 and seq % min(TOKEN_TILE, seq) == 0
    offs = _column_offsets(d)
    h = x.reshape(bsz * seq, d)

    def vec(v):
        return v.reshape(1, -1).astype(F32)

    slope_vec = _replicated_lanes(
        jnp.exp2(-8.0 / C_HEADS * jnp.arange(1, C_HEADS + 1, dtype=F32)))[None, :]

    for l in range(depth):
        lambda_init = 0.8 - 0.6 * math.exp(-0.3 * l)
        wl = w_in[l].astype(BF16)

        def cols(i, j=None):
            return wl[:, offs[i]:offs[i + 1 if j is None else j]]

        w_a = cols(0, 3)
        w_b = cols(3)
        w_c = jnp.concatenate([_per_group(cols(4), C_HEADS), _per_group(cols(5), C_HEADS),
                               _per_group(cols(6), C_HEADS)], axis=-1)
        n_pairs = D_HEADS // 2
        f_slab = jnp.broadcast_to(_replicated_lanes(cols(10))[None], (n_pairs, d, LANES))
        w_d = jnp.concatenate([_per_group(cols(7), n_pairs), _per_group(cols(8), n_pairs),
                               _per_group(cols(9), n_pairs), f_slab], axis=-1)
        w_g = cols(11)

        h, u = _ffn(h, vec(ffn1_pre_g[l]), vec(ffn1_post_g[l]), ffn1_w_gate[l].astype(BF16),
                    ffn1_w_up[l].astype(BF16), ffn1_w_down[l].astype(BF16), vec(mix_pre_g[l]))
        oab = _ab(u, seq, w_a, w_b, a_conv_w[l], b_conv_w[l], vec(b_conv_b[l]), vec(b_ln_g[l]),
                  vec(b_ln_b[l]))
        lam_params = jnp.stack([c_lam_q1[l], c_lam_k1[l], c_lam_q2[l], c_lam_k2[l]]).astype(F32)
        oc = _attn_c(u, seq, w_c, slope_vec, lam_params, vec(c_subln_g[l]), lambda_init)
        od = _attn_d(u, seq, w_d, _replicated_lanes(d_forget_b[l].astype(F32))[None, :])
        h = _merge(u, h, oab, oc, od, w_g, a_w_out[l].astype(BF16), b_w_out[l].astype(BF16),
                   c_w_out[l].astype(BF16), d_w_out[l].astype(BF16), w_o[l].astype(BF16),
                   vec(mix_post_g[l]))
        h = _ffn(h, vec(ffn2_pre_g[l]), vec(ffn2_post_g[l]), ffn2_w_gate[l].astype(BF16),
                 ffn2_w_up[l].astype(BF16), ffn2_w_down[l].astype(BF16))
    return h.reshape(bsz, seq, d)
```

```python
import functools
import math

import jax
import jax.numpy as jnp
from jax import lax
from jax.experimental import pallas as pl
from jax.experimental.pallas import tpu as pltpu

F32 = jnp.float32
BF16 = jnp.bfloat16

EPS = 1e-6
FFN_RESIDUAL_SCALE = 0.5
D_FF = 2816
A_WIDTH = 512
A_CONV = 3
B_WIDTH = 512
B_CONV = 31
C_HEADS = 4
C_QK_DIM = 64
C_V_DIM = 128
D_HEADS = 8
D_HEAD_DIM = 64
N_BRANCHES = 4

LANES = 128
SUBLANES = 8
MXU_DIM = 256
VMEM_LIMIT_BYTES = 56 * 1024 * 1024

TOKEN_TILE = 512
FFN_TOKEN_TILE = 1024
SUB_ROWS = 256
Q_BLOCK = MXU_DIM
SCORE_LOOKAHEAD = 2
BIAS_PIECES = 3
A_HIST = SUBLANES
B_HIST = 32
CONV_ROWS = 64
V_ROWS = C_V_DIM + 16
NEG_BIG = -1e30
LOG2_E = math.log2(math.e)


def _rms(x, g):
    return x * lax.rsqrt(jnp.mean(x * x, axis=-1, keepdims=True) + EPS) * g


def _split3(x):
    hi = x.astype(BF16).astype(F32)
    r = x - hi
    mid = r.astype(BF16).astype(F32)
    lo = (r - mid).astype(BF16).astype(F32)
    return hi, mid, lo


def _piece_slab(x, lane):
    hi, mid, lo = _split3(x)
    r = lane % BIAS_PIECES
    return jnp.where(r == 0, hi, jnp.where(r == 1, mid, lo)).astype(BF16)


def _resident(shape):
    nd = len(shape)
    return pl.BlockSpec(shape, lambda *_: (0,) * nd, pipeline_mode=pl.Buffered(1))


def _params(n_axes):
    return pltpu.CompilerParams(
        dimension_semantics=("arbitrary",) * n_axes,
        vmem_limit_bytes=VMEM_LIMIT_BYTES,
    )


def _ffn_kernel(emit_norm, x_ref, pre_g_ref, post_g_ref, wg_ref, wu_ref, wd_ref, *rest):
    if emit_norm:
        next_g_ref, h_ref, u_ref = rest
    else:
        (h_ref,) = rest
    subs = [slice(r, r + SUB_ROWS) for r in range(0, x_ref.shape[0], SUB_ROWS)]
    xs = [x_ref[sl, :] for sl in subs]
    xns = [_rms(x, pre_g_ref[...]).astype(BF16) for x in xs]
    gs, us = [], []
    for xn in xns:
        gs.append(jnp.dot(xn, wg_ref[...], preferred_element_type=F32))
        us.append(jnp.dot(xn, wu_ref[...], preferred_element_type=F32))
    fs = []
    for g, u in zip(gs, us):
        a = (g * jax.nn.sigmoid(g) * u).astype(BF16)
        fs.append(jnp.dot(a, wd_ref[...], preferred_element_type=F32))
    for sl, x, f in zip(subs, xs, fs):
        h = x + FFN_RESIDUAL_SCALE * _rms(f, post_g_ref[...])
        h_ref[sl, :] = h
        if emit_norm:
            u_ref[sl, :] = _rms(h, next_g_ref[...]).astype(BF16)


def _ffn(x, pre_g, post_g, wg, wu, wd, next_g=None):
    t, d = x.shape
    tm = min(FFN_TOKEN_TILE, t)
    emit_norm = next_g is not None
    row = pl.BlockSpec((tm, d), lambda i: (i, 0))
    vec = _resident((1, d))
    in_specs = [row, vec, vec, _resident(wg.shape), _resident(wu.shape), _resident(wd.shape)]
    args = [x, pre_g, post_g, wg, wu, wd]
    out_shape = [jax.ShapeDtypeStruct((t, d), F32)]
    out_specs = [row]
    if emit_norm:
        in_specs.append(vec)
        args.append(next_g)
        out_shape.append(jax.ShapeDtypeStruct((t, d), BF16))
        out_specs.append(row)
    out = pl.pallas_call(
        functools.partial(_ffn_kernel, emit_norm),
        grid=(t // tm,),
        in_specs=in_specs,
        out_specs=out_specs,
        out_shape=out_shape,
        compiler_params=_params(1),
        name="ffn_norm" if emit_norm else "ffn",
    )(*args)
    return out if emit_norm else out[0]


def _causal_conv(buf_ref, hist, w, n_taps, tm):
    chunks = []
    for r0 in range(0, tm, CONV_ROWS):
        acc = None
        for k in range(n_taps):
            start = hist + r0 - (n_taps - 1) + k
            term = buf_ref[start:start + CONV_ROWS, :] * w[k:k + 1, :]
            acc = term if acc is None else acc + term
        chunks.append(acc)
    return jnp.concatenate(chunks, axis=0)


def _causal_conv_aligned(buf_ref, shift_ref, hist, w, n_taps, tm):
    rows = shift_ref.shape[1]
    filled = 0
    chunks = []
    for r0 in range(0, tm, CONV_ROWS):
        need = min(rows, r0 + CONV_ROWS + hist - SUBLANES)
        if need > filled:
            for r in range(1, SUBLANES):
                shift_ref[r - 1, filled:need, :] = buf_ref[filled + r:need + r, :]
            filled = need
        acc = None
        for k in range(n_taps):
            base = hist - (n_taps - 1) + k
            phase = base % SUBLANES
            start = base - phase + r0
            if phase == 0:
                x = buf_ref[start:start + CONV_ROWS, :]
            else:
                x = shift_ref[phase - 1, start:start + CONV_ROWS, :]
            term = x * w[k:k + 1, :]
            acc = term if acc is None else acc + term
        chunks.append(acc)
    return jnp.concatenate(chunks, axis=0)


def _ab_kernel(tiles_per_seq, u_ref, wa_ref, wb_ref, aw_ref, bw_ref, bb_ref, lng_ref, lnb_ref,
               o_ref, zbuf, ybuf, yshift):
    tm = u_ref.shape[0]
    first = (pl.program_id(0) % tiles_per_seq) == 0

    @pl.when(first)
    def _():
        zbuf[0:A_HIST, :] = jnp.zeros((A_HIST, A_WIDTH), F32)
        ybuf[0:B_HIST, :] = jnp.zeros((B_HIST, B_WIDTH), F32)

    @pl.when(jnp.logical_not(first))
    def _():
        zbuf[0:A_HIST, :] = zbuf[tm:tm + A_HIST, :]
        ybuf[0:B_HIST, :] = ybuf[tm:tm + B_HIST, :]

    half = tm // 2
    for r0 in (0, half):
        pb = jnp.dot(u_ref[r0:r0 + half, :], wb_ref[...], preferred_element_type=F32)
        ybuf[B_HIST + r0:B_HIST + r0 + half, :] = (
            pb[:, 0:B_WIDTH] * jax.nn.sigmoid(pb[:, B_WIDTH:2 * B_WIDTH]))
    pa = jnp.dot(u_ref[...], wa_ref[...], preferred_element_type=F32)

    y = _causal_conv_aligned(ybuf, yshift, B_HIST, bw_ref[...], B_CONV, tm) + bb_ref[...]
    mu = jnp.mean(y, axis=-1, keepdims=True)
    yc = y - mu
    var = jnp.mean(yc * yc, axis=-1, keepdims=True)
    yn = yc * lax.rsqrt(var + EPS) * lng_ref[...] + lnb_ref[...]
    o_b = yn * jax.nn.sigmoid(yn)
    o_ref[:, A_WIDTH:A_WIDTH + B_WIDTH] = o_b.astype(BF16)

    gate_b = pa[:, 0:A_WIDTH]
    zbuf[A_HIST:A_HIST + tm, :] = pa[:, A_WIDTH:2 * A_WIDTH] * pa[:, 2 * A_WIDTH:3 * A_WIDTH]
    o_a = gate_b * _causal_conv(zbuf, A_HIST, aw_ref[...], A_CONV, tm)
    o_ref[:, 0:A_WIDTH] = o_a.astype(BF16)


def _ab(u, seq, wa, wb, a_conv_w, b_conv_w, b_conv_b, b_ln_g, b_ln_b):
    t, d = u.shape
    tm = min(TOKEN_TILE, seq)
    return pl.pallas_call(
        functools.partial(_ab_kernel, seq // tm),
        grid=(t // tm,),
        in_specs=[
            pl.BlockSpec((tm, d), lambda i: (i, 0)),
            _resident(wa.shape), _resident(wb.shape),
            _resident(a_conv_w.shape), _resident(b_conv_w.shape),
            _resident(b_conv_b.shape), _resident(b_ln_g.shape), _resident(b_ln_b.shape),
        ],
        out_specs=pl.BlockSpec((tm, A_WIDTH + B_WIDTH), lambda i: (i, 0)),
        out_shape=jax.ShapeDtypeStruct((t, A_WIDTH + B_WIDTH), BF16),
        scratch_shapes=[
            pltpu.VMEM((A_HIST + tm, A_WIDTH), F32),
            pltpu.VMEM((B_HIST + tm, B_WIDTH), F32),
            pltpu.VMEM((SUBLANES - 1, B_HIST + tm - SUBLANES, B_WIDTH), F32),
        ],
        compiler_params=_params(1),
        name="conv_branches",
    )(u, wa, wb, a_conv_w, b_conv_w, b_conv_b, b_ln_g, b_ln_b)


def _scores(kaug_ref, qx_ref, j):
    w = (j + 1) * Q_BLOCK
    return lax.dot_general(kaug_ref[0:w, :], qx_ref[j], (((1,), (1,)), ((), ())),
                           preferred_element_type=F32)


def _weighted_values(s, vt_ref, mask_bias, j):
    q0 = j * Q_BLOCK
    w = q0 + Q_BLOCK
    s_diag = s[q0:w, :] + mask_bias
    m = jnp.max(s_diag, axis=0, keepdims=True)
    if j > 0:
        s_top = s[0:q0, :]
        m = jnp.maximum(m, jnp.max(s_top, axis=0, keepdims=True))
        p = jnp.concatenate([jnp.exp2(s_top - m), jnp.exp2(s_diag - m)], axis=0)
    else:
        p = jnp.exp2(s_diag - m)
    return jnp.dot(vt_ref[:, 0:w], p.astype(BF16), preferred_element_type=F32)


def _attend_all(kaug_ref, qx_ref, vt_ref, n_blocks, emit):
    key = lax.broadcasted_iota(jnp.int32, (Q_BLOCK, 2 * Q_BLOCK), 0)
    qry = lax.broadcasted_iota(jnp.int32, (Q_BLOCK, 2 * Q_BLOCK), 1) % Q_BLOCK
    mask_bias = jnp.where(key <= qry, 0.0, NEG_BIG).astype(F32)
    pending = [_scores(kaug_ref, qx_ref, j) for j in range(min(SCORE_LOOKAHEAD, n_blocks))]
    for j in range(n_blocks):
        s = pending.pop(0)
        if j + SCORE_LOOKAHEAD < n_blocks:
            pending.append(_scores(kaug_ref, qx_ref, j + SCORE_LOOKAHEAD))
        o = _weighted_values(s, vt_ref, mask_bias, j)
        o0 = o[0:C_V_DIM, 0:Q_BLOCK] / o[C_V_DIM:C_V_DIM + 1, 0:Q_BLOCK]
        o1 = o[0:C_V_DIM, Q_BLOCK:2 * Q_BLOCK] / o[C_V_DIM:C_V_DIM + 1, Q_BLOCK:2 * Q_BLOCK]
        emit(j, o0, o1)


def _attn_proj_kernel(u_ref, wqk_ref, wvt_ref, qk_ref, f_ref, vt_ref):
    u = u_ref[...]
    n_qk = qk_ref.shape[1]
    n_q = n_qk // 2
    p = jnp.dot(u, wqk_ref[...], preferred_element_type=F32)
    qk_ref[:, 0:n_q] = (p[:, 0:n_q] * (C_QK_DIM ** -0.5 * LOG2_E)).astype(BF16)
    qk_ref[:, n_q:n_qk] = p[:, n_q:n_qk].astype(BF16)
    f_ref[...] = p[:, n_qk:n_qk + LANES]
    vt = lax.dot_general(wvt_ref[...], u, (((1,), (1,)), ((), ())), preferred_element_type=F32)
    vt_ref[0] = vt.astype(BF16)


def _attn_proj(u, seq, w_qk, w_vt):
    t, d = u.shape
    tm = min(FFN_TOKEN_TILE, seq)
    tiles_per_seq = seq // tm
    n_qk = w_qk.shape[1] - LANES
    n_v = w_vt.shape[0]
    return pl.pallas_call(
        _attn_proj_kernel,
        grid=(t // tm,),
        in_specs=[pl.BlockSpec((tm, d), lambda i: (i, 0)), _resident(w_qk.shape),
                  _resident(w_vt.shape)],
        out_specs=[
            pl.BlockSpec((tm, n_qk), lambda i: (i, 0)),
            pl.BlockSpec((tm, LANES), lambda i: (i, 0)),
            pl.BlockSpec((1, n_v, tm), lambda i: (i // tiles_per_seq, 0, i % tiles_per_seq)),
        ],
        out_shape=[
            jax.ShapeDtypeStruct((t, n_qk), BF16),
            jax.ShapeDtypeStruct((t, LANES), F32),
            jax.ShapeDtypeStruct((t // seq, n_v, seq), BF16),
        ],
        compiler_params=_params(1),
        name="attention_projection",
    )(u, w_qk, w_vt)


def _load_operands(q_ref, k_ref, vt_ref, kaug, qx, vt, split):
    n_blocks = qx.shape[0]
    lane = lax.broadcasted_iota(jnp.int32, (1, LANES), 1)
    q = q_ref[...]
    low = jnp.where(lane < split, 1.0, 0.0).astype(BF16)
    qx[:, 0:Q_BLOCK, 0:LANES] = (q * low).reshape(n_blocks, Q_BLOCK, LANES)
    qx[:, Q_BLOCK:2 * Q_BLOCK, 0:LANES] = (q * (1.0 - low)).reshape(n_blocks, Q_BLOCK, LANES)
    kaug[:, 0:LANES] = k_ref[...]
    vt[0:C_V_DIM, :] = vt_ref[0]


def _init_vt_tail(vt_ref):
    seq = vt_ref.shape[1]
    row = lax.broadcasted_iota(jnp.int32, (V_ROWS - C_V_DIM, seq), 0)
    vt_ref[C_V_DIM:V_ROWS, :] = jnp.where(row == 0, 1.0, 0.0).astype(BF16)


def _store_bias_selectors(qx_ref, first_lane0, first_lane1):
    lane = lax.broadcasted_iota(jnp.int32, (1, LANES), 1)
    for half, first in enumerate((first_lane0, first_lane1)):
        hit = jnp.logical_and(lane >= first, lane < first + BIAS_PIECES)
        slab = jnp.broadcast_to(jnp.where(hit, 1.0, 0.0), (Q_BLOCK, LANES)).astype(BF16)
        for j in range(qx_ref.shape[0]):
            qx_ref[j, half * Q_BLOCK:(half + 1) * Q_BLOCK, LANES:2 * LANES] = slab


def _attn_c_kernel(lambda_init, q_ref, k_ref, vt_ref, slope_ref, lam_ref, subg_ref, o_ref,
                   kaug, qx, vt):
    seq = q_ref.shape[0]
    b = pl.program_id(0)
    head = pl.program_id(1)

    @pl.when(jnp.logical_and(b == 0, head == 0))
    def _():
        _init_vt_tail(vt)
        pos = lax.broadcasted_iota(jnp.int32, (seq, LANES), 0).astype(F32)
        lane = lax.broadcasted_iota(jnp.int32, (seq, LANES), 1)
        kaug[:, LANES:2 * LANES] = _piece_slab(pos * (slope_ref[...] * LOG2_E), lane)

    _store_bias_selectors(qx, BIAS_PIECES * head, BIAS_PIECES * head)
    _load_operands(q_ref, k_ref, vt_ref, kaug, qx, vt, C_QK_DIM)

    lp = lam_ref[...]
    lam = (jnp.exp(jnp.sum(lp[0:1, :] * lp[1:2, :], keepdims=True))
           - jnp.exp(jnp.sum(lp[2:3, :] * lp[3:4, :], keepdims=True)) + lambda_init)

    def emit(j, o0, o1):
        o = (o0 - lam * o1).T
        o_ref[j * Q_BLOCK:(j + 1) * Q_BLOCK, :] = (
            _rms(o, subg_ref[...]) * (1.0 - lambda_init)).astype(BF16)

    _attend_all(kaug, qx, vt, seq // Q_BLOCK, emit)


def _attn_d_kernel(q_ref, k_ref, vt_ref, f_ref, fb_ref, o_ref, kaug, qx, vt):
    seq = q_ref.shape[0]
    b = pl.program_id(0)
    pair = pl.program_id(1)

    @pl.when(jnp.logical_and(b == 0, pair == 0))
    def _():
        _init_vt_tail(vt)

    @pl.when(pair == 0)
    def _():
        x = f_ref[...] + fb_ref[...]
        logf = jnp.minimum(x, 0.0) - jnp.log1p(jnp.exp(-jnp.abs(x)))
        row = lax.broadcasted_iota(jnp.int32, (seq, LANES), 0)
        c = logf
        shift = 1
        while shift < seq:
            c = c + jnp.where(row >= shift, pltpu.roll(c, shift, axis=0), 0.0)
            shift *= 2
        lane = lax.broadcasted_iota(jnp.int32, (seq, LANES), 1)
        kaug[:, LANES:2 * LANES] = _piece_slab(c * (-LOG2_E), lane)

    _store_bias_selectors(qx, BIAS_PIECES * 2 * pair, BIAS_PIECES * (2 * pair + 1))
    _load_operands(q_ref, k_ref, vt_ref, kaug, qx, vt, D_HEAD_DIM)

    vrow = lax.broadcasted_iota(jnp.int32, (C_V_DIM, Q_BLOCK), 0)

    def emit(j, o0, o1):
        o_ref[j * Q_BLOCK:(j + 1) * Q_BLOCK, :] = (
            jnp.where(vrow < D_HEAD_DIM, o0, o1).T.astype(BF16))

    _attend_all(kaug, qx, vt, seq // Q_BLOCK, emit)


def _attn_scratch(seq):
    return [
        pltpu.VMEM((seq, 2 * LANES), BF16),
        pltpu.VMEM((seq // Q_BLOCK, 2 * Q_BLOCK, 2 * LANES), BF16),
        pltpu.VMEM((V_ROWS, seq), BF16),
    ]


def _head_blocks(seq, q_block0, k_block0, v_block0):
    return [
        pl.BlockSpec((seq, LANES), lambda b, g: (b, q_block0 + g)),
        pl.BlockSpec((seq, LANES), lambda b, g: (b, k_block0 + g)),
        pl.BlockSpec((1, C_V_DIM, seq), lambda b, g: (b, v_block0 + g, 0)),
    ]


def _attn_c(qk, vt_all, seq, slope_vec, lam_params, subln_g, lambda_init):
    t = qk.shape[0]
    n_q_blocks = qk.shape[1] // (2 * LANES)
    return pl.pallas_call(
        functools.partial(_attn_c_kernel, lambda_init),
        grid=(t // seq, C_HEADS),
        in_specs=_head_blocks(seq, 0, n_q_blocks, 0) + [
            _resident(slope_vec.shape), _resident(lam_params.shape), _resident(subln_g.shape)],
        out_specs=pl.BlockSpec((seq, LANES), lambda b, h: (b, h)),
        out_shape=jax.ShapeDtypeStruct((t, C_HEADS * C_V_DIM), BF16),
        scratch_shapes=_attn_scratch(seq),
        compiler_params=_params(2),
        name="diff_attention",
    )(qk, qk, vt_all, slope_vec, lam_params, subln_g)


def _attn_d(qk, vt_all, f_logits, seq, fbias_vec):
    t = qk.shape[0]
    n_pairs = D_HEADS // 2
    n_q_blocks = qk.shape[1] // (2 * LANES)
    return pl.pallas_call(
        _attn_d_kernel,
        grid=(t // seq, n_pairs),
        in_specs=_head_blocks(seq, C_HEADS, n_q_blocks + C_HEADS, C_HEADS) + [
            pl.BlockSpec((seq, LANES), lambda b, p: (b, 0)), _resident(fbias_vec.shape)],
        out_specs=pl.BlockSpec((seq, LANES), lambda b, p: (b, p)),
        out_shape=jax.ShapeDtypeStruct((t, D_HEADS * D_HEAD_DIM), BF16),
        scratch_shapes=_attn_scratch(seq),
        compiler_params=_params(2),
        name="forget_attention",
    )(qk, qk, vt_all, f_logits, fbias_vec)


def _merge_kernel(u_ref, h_ref, oab_ref, oc_ref, od_ref, wg_ref, wa_ref, wb_ref, wc_ref, wd_ref,
                  wo_ref, post_g_ref, out_ref):
    d = h_ref.shape[1]
    wos = (wa_ref, wb_ref, wc_ref, wd_ref)
    subs = [slice(r, r + SUB_ROWS) for r in range(0, h_ref.shape[0], SUB_ROWS)]
    merged = []
    for sl in subs:
        u = u_ref[sl, :]
        outs = (oab_ref[sl, 0:A_WIDTH], oab_ref[sl, A_WIDTH:A_WIDTH + B_WIDTH], oc_ref[sl, :],
                od_ref[sl, :])
        acc = None
        for i, (o, w_ref) in enumerate(zip(outs, wos)):
            gate = jax.nn.sigmoid(
                jnp.dot(u, wg_ref[:, i * d:(i + 1) * d], preferred_element_type=F32))
            term = gate * jnp.dot(o, w_ref[...], preferred_element_type=F32)
            acc = term if acc is None else acc + term
        merged.append(acc.astype(BF16))
    ys = [jnp.dot(m, wo_ref[...], preferred_element_type=F32) for m in merged]
    for sl, y in zip(subs, ys):
        out_ref[sl, :] = h_ref[sl, :] + _rms(y, post_g_ref[...])


def _merge(u, h, oab, oc, od, wg, wa, wb, wc, wd, wo, post_g):
    t, d = h.shape
    tm = min(TOKEN_TILE, t)

    def row(width):
        return pl.BlockSpec((tm, width), lambda i: (i, 0))

    return pl.pallas_call(
        _merge_kernel,
        grid=(t // tm,),
        in_specs=[
            row(d), row(d), row(oab.shape[1]), row(oc.shape[1]), row(od.shape[1]),
            _resident(wg.shape), _resident(wa.shape), _resident(wb.shape), _resident(wc.shape),
            _resident(wd.shape), _resident(wo.shape), _resident(post_g.shape),
        ],
        out_specs=row(d),
        out_shape=jax.ShapeDtypeStruct((t, d), F32),
        compiler_params=_params(1),
        name="gated_merge",
    )(u, h, oab, oc, od, wg, wa, wb, wc, wd, wo, post_g)


def _column_offsets(d_model):
    sizes = (A_WIDTH, A_WIDTH, A_WIDTH, 2 * B_WIDTH,
             C_HEADS * 2 * C_QK_DIM, C_HEADS * 2 * C_QK_DIM, C_HEADS * C_V_DIM,
             D_HEADS * D_HEAD_DIM, D_HEADS * D_HEAD_DIM, D_HEADS * D_HEAD_DIM, D_HEADS,
             N_BRANCHES * d_model)
    offs = [0]
    for s in sizes:
        offs.append(offs[-1] + s)
    return offs


def _replicated_lanes(vals):
    rep = jnp.repeat(vals, BIAS_PIECES, axis=-1)
    pad = [(0, 0)] * (rep.ndim - 1) + [(0, LANES - rep.shape[-1])]
    return jnp.pad(rep, pad)


def kernel(x, ffn1_pre_g, ffn1_post_g, ffn1_w_gate, ffn1_w_up, ffn1_w_down, mix_pre_g, mix_post_g, w_in, a_conv_w, a_w_out, b_conv_w, b_conv_b, b_ln_g, b_ln_b, b_w_out, c_lam_q1, c_lam_k1, c_lam_q2, c_lam_k2, c_subln_g, c_w_out, d_forget_b, d_w_out, w_o, ffn2_pre_g, ffn2_post_g, ffn2_w_gate, ffn2_w_up, ffn2_w_down):
    bsz, seq, d = x.shape
    depth = w_in.shape[0]
    assert seq % (2 * Q_BLOCK) == 0 and seq % min(TOKEN_TILE, seq) == 0
    offs = _column_offsets(d)
    h = x.reshape(bsz * seq, d)

    def vec(v):
        return v.reshape(1, -1).astype(F32)

    slope_vec = _replicated_lanes(
        jnp.exp2(-8.0 / C_HEADS * jnp.arange(1, C_HEADS + 1, dtype=F32)))[None, :]

    for l in range(depth):
        lambda_init = 0.8 - 0.6 * math.exp(-0.3 * l)
        wl = w_in[l].astype(BF16)

        def cols(i, j=None):
            return wl[:, offs[i]:offs[i + 1 if j is None else j]]

        w_a = cols(0, 3)
        w_b = cols(3)
        w_qk = jnp.concatenate([cols(4), cols(7), cols(5), cols(8), _replicated_lanes(cols(10))],
                               axis=1)
        w_vt = jnp.concatenate([cols(6), cols(9)], axis=1).T
        w_g = cols(11)

        h, u = _ffn(h, vec(ffn1_pre_g[l]), vec(ffn1_post_g[l]), ffn1_w_gate[l].astype(BF16),
                    ffn1_w_up[l].astype(BF16), ffn1_w_down[l].astype(BF16), vec(mix_pre_g[l]))
        oab = _ab(u, seq, w_a, w_b, a_conv_w[l], b_conv_w[l], vec(b_conv_b[l]), vec(b_ln_g[l]),
                  vec(b_ln_b[l]))
        lam_params = jnp.stack([c_lam_q1[l], c_lam_k1[l], c_lam_q2[l], c_lam_k2[l]]).astype(F32)
        qk, f_logits, vt_all = _attn_proj(u, seq, w_qk, w_vt)
        oc = _attn_c(qk, vt_all, seq, slope_vec, lam_params, vec(c_subln_g[l]), lambda_init)
        od = _attn_d(qk, vt_all, f_logits, seq,
                     _replicated_lanes(d_forget_b[l].astype(F32))[None, :])
        h = _merge(u, h, oab, oc, od, w_g, a_w_out[l].astype(BF16), b_w_out[l].astype(BF16),
                   c_w_out[l].astype(BF16), d_w_out[l].astype(BF16), w_o[l].astype(BF16),
                   vec(mix_post_g[l]))
        h = _ffn(h, vec(ffn2_pre_g[l]), vec(ffn2_post_g[l]), ffn2_w_gate[l].astype(BF16),
                 ffn2_w_up[l].astype(BF16), ffn2_w_down[l].astype(BF16))
    return h.reshape(bsz, seq, d)
```

```python
import functools
import math

import jax
import jax.numpy as jnp
from jax import lax
from jax.experimental import pallas as pl
from jax.experimental.pallas import tpu as pltpu

F32 = jnp.float32
BF16 = jnp.bfloat16

EPS = 1e-6
FFN_RESIDUAL_SCALE = 0.5
D_FF = 2816
A_WIDTH = 512
A_CONV = 3
B_WIDTH = 512
B_CONV = 31
C_HEADS = 4
C_QK_DIM = 64
C_V_DIM = 128
D_HEADS = 8
D_HEAD_DIM = 64
N_BRANCHES = 4

LANES = 128
SUBLANES = 8
MXU_DIM = 256
VMEM_LIMIT_BYTES = 56 * 1024 * 1024

TOKEN_TILE = 512
FFN_TOKEN_TILE = 1024
SUB_ROWS = 256
Q_BLOCK = MXU_DIM
SCORE_LOOKAHEAD = 2
BIAS_PIECES = 3
A_HIST = SUBLANES
B_HIST = 32
CONV_ROWS = 64
V_ROWS = C_V_DIM + 16
NEG_BIG = -1e30
SELF_SHIFT = 60.0
FAST_RANGE = 2.0 ** 100
LOG2_E = math.log2(math.e)


def _rms(x, g):
    return x * lax.rsqrt(jnp.mean(x * x, axis=-1, keepdims=True) + EPS) * g


def _split3(x):
    hi = x.astype(BF16).astype(F32)
    r = x - hi
    mid = r.astype(BF16).astype(F32)
    lo = (r - mid).astype(BF16).astype(F32)
    return hi, mid, lo


def _piece_slab(x, lane):
    hi, mid, lo = _split3(x)
    r = lane % BIAS_PIECES
    return jnp.where(r == 0, hi, jnp.where(r == 1, mid, lo)).astype(BF16)


def _resident(shape):
    nd = len(shape)
    return pl.BlockSpec(shape, lambda *_: (0,) * nd, pipeline_mode=pl.Buffered(1))


def _params(n_axes):
    return pltpu.CompilerParams(
        dimension_semantics=("arbitrary",) * n_axes,
        vmem_limit_bytes=VMEM_LIMIT_BYTES,
    )


def _ffn_kernel(emit_norm, x_ref, pre_g_ref, post_g_ref, wg_ref, wu_ref, wd_ref, *rest):
    if emit_norm:
        next_g_ref, h_ref, u_ref = rest
    else:
        (h_ref,) = rest
    subs = [slice(r, r + SUB_ROWS) for r in range(0, x_ref.shape[0], SUB_ROWS)]
    xs = [x_ref[sl, :] for sl in subs]
    xns = [_rms(x, pre_g_ref[...]).astype(BF16) for x in xs]
    gs, us = [], []
    for xn in xns:
        gs.append(jnp.dot(xn, wg_ref[...], preferred_element_type=F32))
        us.append(jnp.dot(xn, wu_ref[...], preferred_element_type=F32))
    fs = []
    for g, u in zip(gs, us):
        a = (g * jax.nn.sigmoid(g) * u).astype(BF16)
        fs.append(jnp.dot(a, wd_ref[...], preferred_element_type=F32))
    for sl, x, f in zip(subs, xs, fs):
        h = x + FFN_RESIDUAL_SCALE * _rms(f, post_g_ref[...])
        h_ref[sl, :] = h
        if emit_norm:
            u_ref[sl, :] = _rms(h, next_g_ref[...]).astype(BF16)


def _ffn(x, pre_g, post_g, wg, wu, wd, next_g=None):
    t, d = x.shape
    tm = min(FFN_TOKEN_TILE, t)
    emit_norm = next_g is not None
    row = pl.BlockSpec((tm, d), lambda i: (i, 0))
    vec = _resident((1, d))
    in_specs = [row, vec, vec, _resident(wg.shape), _resident(wu.shape), _resident(wd.shape)]
    args = [x, pre_g, post_g, wg, wu, wd]
    out_shape = [jax.ShapeDtypeStruct((t, d), F32)]
    out_specs = [row]
    if emit_norm:
        in_specs.append(vec)
        args.append(next_g)
        out_shape.append(jax.ShapeDtypeStruct((t, d), BF16))
        out_specs.append(row)
    out = pl.pallas_call(
        functools.partial(_ffn_kernel, emit_norm),
        grid=(t // tm,),
        in_specs=in_specs,
        out_specs=out_specs,
        out_shape=out_shape,
        compiler_params=_params(1),
        name="ffn_norm" if emit_norm else "ffn",
    )(*args)
    return out if emit_norm else out[0]


def _causal_conv(buf_ref, hist, w, n_taps, tm):
    chunks = []
    for r0 in range(0, tm, CONV_ROWS):
        acc = None
        for k in range(n_taps):
            start = hist + r0 - (n_taps - 1) + k
            term = buf_ref[start:start + CONV_ROWS, :] * w[k:k + 1, :]
            acc = term if acc is None else acc + term
        chunks.append(acc)
    return jnp.concatenate(chunks, axis=0)


def _causal_conv_aligned(buf_ref, shift_ref, hist, w, n_taps, tm):
    rows = shift_ref.shape[1]
    filled = 0
    chunks = []
    for r0 in range(0, tm, CONV_ROWS):
        need = min(rows, r0 + CONV_ROWS + hist - SUBLANES)
        if need > filled:
            for r in range(1, SUBLANES):
                shift_ref[r - 1, filled:need, :] = buf_ref[filled + r:need + r, :]
            filled = need
        acc = None
        for k in range(n_taps):
            base = hist - (n_taps - 1) + k
            phase = base % SUBLANES
            start = base - phase + r0
            if phase == 0:
                x = buf_ref[start:start + CONV_ROWS, :]
            else:
                x = shift_ref[phase - 1, start:start + CONV_ROWS, :]
            term = x * w[k:k + 1, :]
            acc = term if acc is None else acc + term
        chunks.append(acc)
    return jnp.concatenate(chunks, axis=0)


def _ab_kernel(tiles_per_seq, u_ref, wa_ref, wb_ref, aw_ref, bw_ref, bb_ref, lng_ref, lnb_ref,
               o_ref, zbuf, ybuf, yshift):
    tm = u_ref.shape[0]
    first = (pl.program_id(0) % tiles_per_seq) == 0

    @pl.when(first)
    def _():
        zbuf[0:A_HIST, :] = jnp.zeros((A_HIST, A_WIDTH), F32)
        ybuf[0:B_HIST, :] = jnp.zeros((B_HIST, B_WIDTH), F32)

    @pl.when(jnp.logical_not(first))
    def _():
        zbuf[0:A_HIST, :] = zbuf[tm:tm + A_HIST, :]
        ybuf[0:B_HIST, :] = ybuf[tm:tm + B_HIST, :]

    half = tm // 2
    for r0 in (0, half):
        pb = jnp.dot(u_ref[r0:r0 + half, :], wb_ref[...], preferred_element_type=F32)
        ybuf[B_HIST + r0:B_HIST + r0 + half, :] = (
            pb[:, 0:B_WIDTH] * jax.nn.sigmoid(pb[:, B_WIDTH:2 * B_WIDTH]))
    pa = jnp.dot(u_ref[...], wa_ref[...], preferred_element_type=F32)

    y = _causal_conv_aligned(ybuf, yshift, B_HIST, bw_ref[...], B_CONV, tm) + bb_ref[...]
    mu = jnp.mean(y, axis=-1, keepdims=True)
    yc = y - mu
    var = jnp.mean(yc * yc, axis=-1, keepdims=True)
    yn = yc * lax.rsqrt(var + EPS) * lng_ref[...] + lnb_ref[...]
    o_b = yn * jax.nn.sigmoid(yn)
    o_ref[:, A_WIDTH:A_WIDTH + B_WIDTH] = o_b.astype(BF16)

    gate_b = pa[:, 0:A_WIDTH]
    zbuf[A_HIST:A_HIST + tm, :] = pa[:, A_WIDTH:2 * A_WIDTH] * pa[:, 2 * A_WIDTH:3 * A_WIDTH]
    o_a = gate_b * _causal_conv(zbuf, A_HIST, aw_ref[...], A_CONV, tm)
    o_ref[:, 0:A_WIDTH] = o_a.astype(BF16)


def _ab(u, seq, wa, wb, a_conv_w, b_conv_w, b_conv_b, b_ln_g, b_ln_b):
    t, d = u.shape
    tm = min(TOKEN_TILE, seq)
    return pl.pallas_call(
        functools.partial(_ab_kernel, seq // tm),
        grid=(t // tm,),
        in_specs=[
            pl.BlockSpec((tm, d), lambda i: (i, 0)),
            _resident(wa.shape), _resident(wb.shape),
            _resident(a_conv_w.shape), _resident(b_conv_w.shape),
            _resident(b_conv_b.shape), _resident(b_ln_g.shape), _resident(b_ln_b.shape),
        ],
        out_specs=pl.BlockSpec((tm, A_WIDTH + B_WIDTH), lambda i: (i, 0)),
        out_shape=jax.ShapeDtypeStruct((t, A_WIDTH + B_WIDTH), BF16),
        scratch_shapes=[
            pltpu.VMEM((A_HIST + tm, A_WIDTH), F32),
            pltpu.VMEM((B_HIST + tm, B_WIDTH), F32),
            pltpu.VMEM((SUBLANES - 1, B_HIST + tm - SUBLANES, B_WIDTH), F32),
        ],
        compiler_params=_params(1),
        name="conv_branches",
    )(u, wa, wb, a_conv_w, b_conv_w, b_conv_b, b_ln_g, b_ln_b)


def _scores(kaug_ref, qx_ref, j):
    w = (j + 1) * Q_BLOCK
    return lax.dot_general(kaug_ref[0:w, :], qx_ref[j], (((1,), (1,)), ((), ())),
                           preferred_element_type=F32)


def _weighted_values(s, vt_ref, mask_bias, j):
    q0 = j * Q_BLOCK
    w = q0 + Q_BLOCK
    s_diag = s[q0:w, :] + mask_bias
    m = jnp.max(s_diag, axis=0, keepdims=True)
    if j > 0:
        s_top = s[0:q0, :]
        m = jnp.maximum(m, jnp.max(s_top, axis=0, keepdims=True))
        p = jnp.concatenate([jnp.exp2(s_top - m), jnp.exp2(s_diag - m)], axis=0)
    else:
        p = jnp.exp2(s_diag - m)
    return jnp.dot(vt_ref[:, 0:w], p.astype(BF16), preferred_element_type=F32)


def _causal_mask_bias():
    key = lax.broadcasted_iota(jnp.int32, (Q_BLOCK, 2 * Q_BLOCK), 0)
    qry = lax.broadcasted_iota(jnp.int32, (Q_BLOCK, 2 * Q_BLOCK), 1) % Q_BLOCK
    return jnp.where(key <= qry, 0.0, NEG_BIG).astype(F32), key == qry


def _normalised(o):
    o0 = o[0:C_V_DIM, 0:Q_BLOCK] / o[C_V_DIM:C_V_DIM + 1, 0:Q_BLOCK]
    o1 = o[0:C_V_DIM, Q_BLOCK:2 * Q_BLOCK] / o[C_V_DIM:C_V_DIM + 1, Q_BLOCK:2 * Q_BLOCK]
    return o0, o1


def _attend_all(kaug_ref, qx_ref, vt_ref, n_blocks, emit):
    mask_bias, _ = _causal_mask_bias()
    pending = [_scores(kaug_ref, qx_ref, j) for j in range(min(SCORE_LOOKAHEAD, n_blocks))]
    for j in range(n_blocks):
        s = pending.pop(0)
        if j + SCORE_LOOKAHEAD < n_blocks:
            pending.append(_scores(kaug_ref, qx_ref, j + SCORE_LOOKAHEAD))
        emit(j, *_normalised(_weighted_values(s, vt_ref, mask_bias, j)))


def _attend_all_fast(kaug_ref, qx_ref, vt_ref, n_blocks, emit):
    nt = (((1,), (1,)), ((), ()))
    mask_bias, on_diag = _causal_mask_bias()
    unsafe = jnp.zeros((1, 2 * Q_BLOCK), F32)
    waiting = None
    for j in range(n_blocks + 1):
        if j < n_blocks:
            q0 = j * Q_BLOCK
            s_d = lax.dot_general(kaug_ref[q0:q0 + Q_BLOCK, :], qx_ref[j], nt,
                                  preferred_element_type=F32)
            m = jnp.sum(jnp.where(on_diag, s_d, 0.0), axis=0, keepdims=True) + SELF_SHIFT
        if waiting is not None:
            jw, p = waiting
            o = jnp.dot(vt_ref[:, 0:(jw + 1) * Q_BLOCK], p, preferred_element_type=F32)
            denom = o[C_V_DIM:C_V_DIM + 1, :]
            peak = jnp.max(jnp.abs(o), axis=0, keepdims=True)
            ok = jnp.logical_and(denom > 1.0 / FAST_RANGE, peak < FAST_RANGE)
            unsafe = jnp.maximum(unsafe, jnp.where(ok, 0.0, 1.0))
            emit(jw, *_normalised(o))
        if j < n_blocks:
            p = jnp.exp2(s_d + mask_bias - m).astype(BF16)
            if j > 0:
                s_t = lax.dot_general(kaug_ref[0:q0, :], qx_ref[j], nt, preferred_element_type=F32)
                p = jnp.concatenate([jnp.exp2(s_t - m).astype(BF16), p], axis=0)
            waiting = (j, p)
    return unsafe


def _attend(kaug_ref, qx_ref, vt_ref, n_blocks, emit):
    unsafe = _attend_all_fast(kaug_ref, qx_ref, vt_ref, n_blocks, emit)

    @pl.when(jnp.max(unsafe) > 0.0)
    def _():
        _attend_all(kaug_ref, qx_ref, vt_ref, n_blocks, emit)


def _attn_proj_kernel(u_ref, wqk_ref, wvt_ref, qk_ref, f_ref, vt_ref):
    u = u_ref[...]
    n_qk = qk_ref.shape[1]
    n_q = n_qk // 2
    p = jnp.dot(u, wqk_ref[...], preferred_element_type=F32)
    qk_ref[:, 0:n_q] = (p[:, 0:n_q] * (C_QK_DIM ** -0.5 * LOG2_E)).astype(BF16)
    qk_ref[:, n_q:n_qk] = p[:, n_q:n_qk].astype(BF16)
    f_ref[...] = p[:, n_qk:n_qk + LANES]
    vt = lax.dot_general(wvt_ref[...], u, (((1,), (1,)), ((), ())), preferred_element_type=F32)
    vt_ref[0] = vt.astype(BF16)


def _attn_proj(u, seq, w_qk, w_vt):
    t, d = u.shape
    tm = min(FFN_TOKEN_TILE, seq)
    tiles_per_seq = seq // tm
    n_qk = w_qk.shape[1] - LANES
    n_v = w_vt.shape[0]
    return pl.pallas_call(
        _attn_proj_kernel,
        grid=(t // tm,),
        in_specs=[pl.BlockSpec((tm, d), lambda i: (i, 0)), _resident(w_qk.shape),
                  _resident(w_vt.shape)],
        out_specs=[
            pl.BlockSpec((tm, n_qk), lambda i: (i, 0)),
            pl.BlockSpec((tm, LANES), lambda i: (i, 0)),
            pl.BlockSpec((1, n_v, tm), lambda i: (i // tiles_per_seq, 0, i % tiles_per_seq)),
        ],
        out_shape=[
            jax.ShapeDtypeStruct((t, n_qk), BF16),
            jax.ShapeDtypeStruct((t, LANES), F32),
            jax.ShapeDtypeStruct((t // seq, n_v, seq), BF16),
        ],
        compiler_params=_params(1),
        name="attention_projection",
    )(u, w_qk, w_vt)


def _load_operands(q_ref, k_ref, vt_ref, kaug, qx, vt, split):
    n_blocks = qx.shape[0]
    lane = lax.broadcasted_iota(jnp.int32, (1, LANES), 1)
    q = q_ref[...]
    low = jnp.where(lane < split, 1.0, 0.0).astype(BF16)
    qx[:, 0:Q_BLOCK, 0:LANES] = (q * low).reshape(n_blocks, Q_BLOCK, LANES)
    qx[:, Q_BLOCK:2 * Q_BLOCK, 0:LANES] = (q * (1.0 - low)).reshape(n_blocks, Q_BLOCK, LANES)
    kaug[:, 0:LANES] = k_ref[...]
    vt[0:C_V_DIM, :] = vt_ref[0]


def _init_vt_tail(vt_ref):
    seq = vt_ref.shape[1]
    row = lax.broadcasted_iota(jnp.int32, (V_ROWS - C_V_DIM, seq), 0)
    vt_ref[C_V_DIM:V_ROWS, :] = jnp.where(row == 0, 1.0, 0.0).astype(BF16)


def _store_bias_selectors(qx_ref, first_lane0, first_lane1):
    lane = lax.broadcasted_iota(jnp.int32, (1, LANES), 1)
    for half, first in enumerate((first_lane0, first_lane1)):
        hit = jnp.logical_and(lane >= first, lane < first + BIAS_PIECES)
        slab = jnp.broadcast_to(jnp.where(hit, 1.0, 0.0), (Q_BLOCK, LANES)).astype(BF16)
        for j in range(qx_ref.shape[0]):
            qx_ref[j, half * Q_BLOCK:(half + 1) * Q_BLOCK, LANES:2 * LANES] = slab


def _attn_c_kernel(lambda_init, q_ref, k_ref, vt_ref, slope_ref, lam_ref, subg_ref, o_ref,
                   kaug, qx, vt):
    seq = q_ref.shape[0]
    b = pl.program_id(0)
    head = pl.program_id(1)

    @pl.when(jnp.logical_and(b == 0, head == 0))
    def _():
        _init_vt_tail(vt)
        pos = lax.broadcasted_iota(jnp.int32, (seq, LANES), 0).astype(F32)
        lane = lax.broadcasted_iota(jnp.int32, (seq, LANES), 1)
        kaug[:, LANES:2 * LANES] = _piece_slab(pos * (slope_ref[...] * LOG2_E), lane)

    _store_bias_selectors(qx, BIAS_PIECES * head, BIAS_PIECES * head)
    _load_operands(q_ref, k_ref, vt_ref, kaug, qx, vt, C_QK_DIM)

    lp = lam_ref[...]
    lam = (jnp.exp(jnp.sum(lp[0:1, :] * lp[1:2, :], keepdims=True))
           - jnp.exp(jnp.sum(lp[2:3, :] * lp[3:4, :], keepdims=True)) + lambda_init)

    def emit(j, o0, o1):
        o = (o0 - lam * o1).T
        o_ref[j * Q_BLOCK:(j + 1) * Q_BLOCK, :] = (
            _rms(o, subg_ref[...]) * (1.0 - lambda_init)).astype(BF16)

    _attend(kaug, qx, vt, seq // Q_BLOCK, emit)


def _attn_d_kernel(q_ref, k_ref, vt_ref, f_ref, fb_ref, o_ref, kaug, qx, vt):
    seq = q_ref.shape[0]
    b = pl.program_id(0)
    pair = pl.program_id(1)

    @pl.when(jnp.logical_and(b == 0, pair == 0))
    def _():
        _init_vt_tail(vt)

    @pl.when(pair == 0)
    def _():
        x = f_ref[...] + fb_ref[...]
        logf = jnp.minimum(x, 0.0) - jnp.log1p(jnp.exp(-jnp.abs(x)))
        row = lax.broadcasted_iota(jnp.int32, (seq, LANES), 0)
        c = logf
        shift = 1
        while shift < seq:
            c = c + jnp.where(row >= shift, pltpu.roll(c, shift, axis=0), 0.0)
            shift *= 2
        lane = lax.broadcasted_iota(jnp.int32, (seq, LANES), 1)
        kaug[:, LANES:2 * LANES] = _piece_slab(c * (-LOG2_E), lane)

    _store_bias_selectors(qx, BIAS_PIECES * 2 * pair, BIAS_PIECES * (2 * pair + 1))
    _load_operands(q_ref, k_ref, vt_ref, kaug, qx, vt, D_HEAD_DIM)

    vrow = lax.broadcasted_iota(jnp.int32, (C_V_DIM, Q_BLOCK), 0)

    def emit(j, o0, o1):
        o_ref[j * Q_BLOCK:(j + 1) * Q_BLOCK, :] = (
            jnp.where(vrow < D_HEAD_DIM, o0, o1).T.astype(BF16))

    _attend(kaug, qx, vt, seq // Q_BLOCK, emit)


def _attn_scratch(seq):
    return [
        pltpu.VMEM((seq, 2 * LANES), BF16),
        pltpu.VMEM((seq // Q_BLOCK, 2 * Q_BLOCK, 2 * LANES), BF16),
        pltpu.VMEM((V_ROWS, seq), BF16),
    ]


def _head_blocks(seq, q_block0, k_block0, v_block0):
    return [
        pl.BlockSpec((seq, LANES), lambda b, g: (b, q_block0 + g)),
        pl.BlockSpec((seq, LANES), lambda b, g: (b, k_block0 + g)),
        pl.BlockSpec((1, C_V_DIM, seq), lambda b, g: (b, v_block0 + g, 0)),
    ]


def _attn_c(qk, vt_all, seq, slope_vec, lam_params, subln_g, lambda_init):
    t = qk.shape[0]
    n_q_blocks = qk.shape[1] // (2 * LANES)
    return pl.pallas_call(
        functools.partial(_attn_c_kernel, lambda_init),
        grid=(t // seq, C_HEADS),
        in_specs=_head_blocks(seq, 0, n_q_blocks, 0) + [
            _resident(slope_vec.shape), _resident(lam_params.shape), _resident(subln_g.shape)],
        out_specs=pl.BlockSpec((seq, LANES), lambda b, h: (b, h)),
        out_shape=jax.ShapeDtypeStruct((t, C_HEADS * C_V_DIM), BF16),
        scratch_shapes=_attn_scratch(seq),
        compiler_params=_params(2),
        name="diff_attention",
    )(qk, qk, vt_all, slope_vec, lam_params, subln_g)


def _attn_d(qk, vt_all, f_logits, seq, fbias_vec):
    t = qk.shape[0]
    n_pairs = D_HEADS // 2
    n_q_blocks = qk.shape[1] // (2 * LANES)
    return pl.pallas_call(
        _attn_d_kernel,
        grid=(t // seq, n_pairs),
        in_specs=_head_blocks(seq, C_HEADS, n_q_blocks + C_HEADS, C_HEADS) + [
            pl.BlockSpec((seq, LANES), lambda b, p: (b, 0)), _resident(fbias_vec.shape)],
        out_specs=pl.BlockSpec((seq, LANES), lambda b, p: (b, p)),
        out_shape=jax.ShapeDtypeStruct((t, D_HEADS * D_HEAD_DIM), BF16),
        scratch_shapes=_attn_scratch(seq),
        compiler_params=_params(2),
        name="forget_attention",
    )(qk, qk, vt_all, f_logits, fbias_vec)


def _merge_kernel(u_ref, h_ref, oab_ref, oc_ref, od_ref, wg_ref, wa_ref, wb_ref, wc_ref, wd_ref,
                  wo_ref, post_g_ref, out_ref):
    d = h_ref.shape[1]
    wos = (wa_ref, wb_ref, wc_ref, wd_ref)
    subs = [slice(r, r + SUB_ROWS) for r in range(0, h_ref.shape[0], SUB_ROWS)]
    merged = []
    for sl in subs:
        u = u_ref[sl, :]
        outs = (oab_ref[sl, 0:A_WIDTH], oab_ref[sl, A_WIDTH:A_WIDTH + B_WIDTH], oc_ref[sl, :],
                od_ref[sl, :])
        acc = None
        for i, (o, w_ref) in enumerate(zip(outs, wos)):
            gate = jax.nn.sigmoid(
                jnp.dot(u, wg_ref[:, i * d:(i + 1) * d], preferred_element_type=F32))
            term = gate * jnp.dot(o, w_ref[...], preferred_element_type=F32)
            acc = term if acc is None else acc + term
        merged.append(acc.astype(BF16))
    ys = [jnp.dot(m, wo_ref[...], preferred_element_type=F32) for m in merged]
    for sl, y in zip(subs, ys):
        out_ref[sl, :] = h_ref[sl, :] + _rms(y, post_g_ref[...])


def _merge(u, h, oab, oc, od, wg, wa, wb, wc, wd, wo, post_g):
    t, d = h.shape
    tm = min(TOKEN_TILE, t)

    def row(width):
        return pl.BlockSpec((tm, width), lambda i: (i, 0))

    return pl.pallas_call(
        _merge_kernel,
        grid=(t // tm,),
        in_specs=[
            row(d), row(d), row(oab.shape[1]), row(oc.shape[1]), row(od.shape[1]),
            _resident(wg.shape), _resident(wa.shape), _resident(wb.shape), _resident(wc.shape),
            _resident(wd.shape), _resident(wo.shape), _resident(post_g.shape),
        ],
        out_specs=row(d),
        out_shape=jax.ShapeDtypeStruct((t, d), F32),
        compiler_params=_params(1),
        name="gated_merge",
    )(u, h, oab, oc, od, wg, wa, wb, wc, wd, wo, post_g)


def _column_offsets(d_model):
    sizes = (A_WIDTH, A_WIDTH, A_WIDTH, 2 * B_WIDTH,
             C_HEADS * 2 * C_QK_DIM, C_HEADS * 2 * C_QK_DIM, C_HEADS * C_V_DIM,
             D_HEADS * D_HEAD_DIM, D_HEADS * D_HEAD_DIM, D_HEADS * D_HEAD_DIM, D_HEADS,
             N_BRANCHES * d_model)
    offs = [0]
    for s in sizes:
        offs.append(offs[-1] + s)
    return offs


def _replicated_lanes(vals):
    rep = jnp.repeat(vals, BIAS_PIECES, axis=-1)
    pad = [(0, 0)] * (rep.ndim - 1) + [(0, LANES - rep.shape[-1])]
    return jnp.pad(rep, pad)


def kernel(x, ffn1_pre_g, ffn1_post_g, ffn1_w_gate, ffn1_w_up, ffn1_w_down, mix_pre_g, mix_post_g, w_in, a_conv_w, a_w_out, b_conv_w, b_conv_b, b_ln_g, b_ln_b, b_w_out, c_lam_q1, c_lam_k1, c_lam_q2, c_lam_k2, c_subln_g, c_w_out, d_forget_b, d_w_out, w_o, ffn2_pre_g, ffn2_post_g, ffn2_w_gate, ffn2_w_up, ffn2_w_down):
    bsz, seq, d = x.shape
    depth = w_in.shape[0]
    assert seq % (2 * Q_BLOCK) == 0 and seq % min(TOKEN_TILE, seq) == 0
    offs = _column_offsets(d)
    h = x.reshape(bsz * seq, d)

    def vec(v):
        return v.reshape(1, -1).astype(F32)

    slope_vec = _replicated_lanes(
        jnp.exp2(-8.0 / C_HEADS * jnp.arange(1, C_HEADS + 1, dtype=F32)))[None, :]

    for l in range(depth):
        lambda_init = 0.8 - 0.6 * math.exp(-0.3 * l)
        wl = w_in[l].astype(BF16)

        def cols(i, j=None):
            return wl[:, offs[i]:offs[i + 1 if j is None else j]]

        w_a = cols(0, 3)
        w_b = cols(3)
        w_qk = jnp.concatenate([cols(4), cols(7), cols(5), cols(8), _replicated_lanes(cols(10))],
                               axis=1)
        w_vt = jnp.concatenate([cols(6), cols(9)], axis=1).T
        w_g = cols(11)

        h, u = _ffn(h, vec(ffn1_pre_g[l]), vec(ffn1_post_g[l]), ffn1_w_gate[l].astype(BF16),
                    ffn1_w_up[l].astype(BF16), ffn1_w_down[l].astype(BF16), vec(mix_pre_g[l]))
        oab = _ab(u, seq, w_a, w_b, a_conv_w[l], b_conv_w[l], vec(b_conv_b[l]), vec(b_ln_g[l]),
                  vec(b_ln_b[l]))
        lam_params = jnp.stack([c_lam_q1[l], c_lam_k1[l], c_lam_q2[l], c_lam_k2[l]]).astype(F32)
        qk, f_logits, vt_all = _attn_proj(u, seq, w_qk, w_vt)
        oc = _attn_c(qk, vt_all, seq, slope_vec, lam_params, vec(c_subln_g[l]), lambda_init)
        od = _attn_d(qk, vt_all, f_logits, seq,
                     _replicated_lanes(d_forget_b[l].astype(F32))[None, :])
        h = _merge(u, h, oab, oc, od, w_g, a_w_out[l].astype(BF16), b_w_out[l].astype(BF16),
                   c_w_out[l].astype(BF16), d_w_out[l].astype(BF16), w_o[l].astype(BF16),
                   vec(mix_post_g[l]))
        h = _ffn(h, vec(ffn2_pre_g[l]), vec(ffn2_post_g[l]), ffn2_w_gate[l].astype(BF16),
                 ffn2_w_up[l].astype(BF16), ffn2_w_down[l].astype(BF16))
    return h.reshape(bsz, seq, d)
```
